```python
import math
import jax, jax.numpy as jnp
from jax import lax
import numpy as np

D_MODEL = 2048
BATCH = 4
SEQ = 2048
DEPTH = 4

N_MIXERS = 2
POOL_WINDOWS = (2, 4, 8, 16)
N_POOL_GROUPS = len(POOL_WINDOWS)
POOL_GROUP = D_MODEL // N_POOL_GROUPS
HEAD_DIM = 128
N_HEADS = D_MODEL // HEAD_DIM
MOBA_BLOCK = 256
MOBA_TOPK = 3
Q_CHUNK = 16
REL_BUCKETS = 32
REL_MAX_DIST = 128
D_FF = 5632
CONV_WIDTH = 3
EPS = 1e-6
NEG = -1e30
N_POOL_LAYERS = (DEPTH + N_MIXERS - 1) // N_MIXERS
N_MOBA_LAYERS = DEPTH // N_MIXERS

kernel_name = 'hybrid_pool_moba_convffn_adaln'


def rmsnorm(x, g):
    xf = x.astype(jnp.float32)
    y = xf * lax.rsqrt(jnp.mean(xf * xf, axis=-1, keepdims=True) + EPS)
    return (y * g.astype(jnp.float32)).astype(x.dtype)


def modulate(h, shift, scale):
    return h * (1 + scale[:, None, :]) + shift[:, None, :]


def rel_bucket(dist):
    n = jnp.maximum(dist, 0)
    max_exact = REL_BUCKETS // 2
    nf = jnp.maximum(n, 1).astype(jnp.float32)
    large = max_exact + (jnp.log(nf / max_exact) / math.log(REL_MAX_DIST / max_exact)
                         * (REL_BUCKETS - max_exact)).astype(jnp.int32)
    large = jnp.minimum(large, REL_BUCKETS - 1)
    return jnp.where(n < max_exact, n, large)


def pool_mixer(h, w_grp, layer_scale):
    B, S, D = h.shape
    hf = h.astype(jnp.float32)
    cs = jnp.concatenate([jnp.zeros((B, 1, D), jnp.float32), jnp.cumsum(hf, axis=1)], axis=1)
    t = jnp.arange(S)
    outs = []
    for g, w in enumerate(POOL_WINDOWS):
        sl = slice(g * POOL_GROUP, (g + 1) * POOL_GROUP)
        lo = jnp.maximum(t + 1 - w, 0)
        cnt = (t + 1 - lo).astype(jnp.float32)
        mean = (cs[:, 1:, sl] - cs[:, lo, sl]) / cnt[None, :, None]
        d = (mean - hf[:, :, sl]).astype(h.dtype)
        outs.append(d @ w_grp[g])
    return jnp.concatenate(outs, axis=-1) * layer_scale


def moba_attention(h, w_qkv, w_o, rel_table):
    B, S, D = h.shape
    nb = -(-S // MOBA_BLOCK)
    sp = nb * MOBA_BLOCK
    topk = min(MOBA_TOPK, nb)
    qkv = jnp.pad(h @ w_qkv, ((0, 0), (0, sp - S), (0, 0)))
    q, k, v = jnp.split(qkv, 3, axis=-1)

    def heads(a):
        return a.reshape(B, sp, N_HEADS, HEAD_DIM).transpose(0, 2, 1, 3)

    q = heads(q) * (HEAD_DIM ** -0.5)
    k, v = heads(k), heads(v)
    kb = k.reshape(B, N_HEADS, nb, MOBA_BLOCK, HEAD_DIM)
    vb = v.reshape(B, N_HEADS, nb, MOBA_BLOCK, HEAD_DIM)
    kmean = jnp.mean(kb.astype(jnp.float32), axis=3)
    qblk = jnp.arange(sp) // MOBA_BLOCK
    gate = jnp.einsum('bhsd,bhnd->bhsn', q.astype(jnp.float32), kmean)
    past = jnp.arange(nb)[None, :] < qblk[:, None]
    gate = jnp.where(past, gate, NEG)
    _, idx = lax.top_k(gate, topk)

    nq = sp // Q_CHUNK
    q_c = q.reshape(B, N_HEADS, nq, Q_CHUNK, HEAD_DIM).transpose(2, 0, 1, 3, 4)
    idx_c = idx.reshape(B, N_HEADS, nq, Q_CHUNK, topk).transpose(2, 0, 1, 3, 4)
    head_ix = jnp.arange(N_HEADS)[None, :, None, None, None]
    s_ix = jnp.arange(MOBA_BLOCK)

    def chunk(args):
        ci, qc, ic = args
        t = ci * Q_CHUNK + jnp.arange(Q_CHUNK)
        j = (ci * Q_CHUNK) // MOBA_BLOCK
        flat = ic.reshape(B, N_HEADS, Q_CHUNK * topk)[..., None, None]
        ks = jnp.take_along_axis(kb, flat, axis=2).reshape(B, N_HEADS, Q_CHUNK, topk, MOBA_BLOCK, HEAD_DIM)
        vs = jnp.take_along_axis(vb, flat, axis=2).reshape(B, N_HEADS, Q_CHUNK, topk, MOBA_BLOCK, HEAD_DIM)
        l_sel = jnp.einsum('bhqd,bhqksd->bhqks', qc, ks).astype(jnp.float32)
        dist_sel = t[:, None, None] - ic[..., None] * MOBA_BLOCK - s_ix
        l_sel = l_sel + rel_table[rel_bucket(dist_sel), head_ix].astype(jnp.float32)
        valid = ic < j
        l_sel = jnp.where(valid[..., None], l_sel, NEG).reshape(B, N_HEADS, Q_CHUNK, topk * MOBA_BLOCK)
        ko = lax.dynamic_index_in_dim(kb, j, axis=2, keepdims=False)
        vo = lax.dynamic_index_in_dim(vb, j, axis=2, keepdims=False)
        l_own = jnp.einsum('bhqd,bhsd->bhqs', qc, ko).astype(jnp.float32)
        dist_own = t[:, None] - (j * MOBA_BLOCK + s_ix)[None, :]
        l_own = l_own + rel_table[rel_bucket(dist_own)].astype(jnp.float32).transpose(2, 0, 1)[None]
        l_own = jnp.where((dist_own >= 0)[None, None], l_own, NEG)
        p = jax.nn.softmax(jnp.concatenate([l_sel, l_own], axis=-1), axis=-1).astype(v.dtype)
        p_sel = p[..., :topk * MOBA_BLOCK].reshape(B, N_HEADS, Q_CHUNK, topk, MOBA_BLOCK)
        p_own = p[..., topk * MOBA_BLOCK:]
        return (jnp.einsum('bhqks,bhqksd->bhqd', p_sel, vs)
                + jnp.einsum('bhqs,bhsd->bhqd', p_own, vo))

    o = lax.map(chunk, (jnp.arange(nq), q_c, idx_c))
    o = o.transpose(1, 0, 3, 2, 4).reshape(B, sp, D)[:, :S]
    return o @ w_o


def conv_ffn(h, w_up, conv_w, conv_b, w_down):
    u = h @ w_up
    C = u.shape[-1]
    u = lax.conv_general_dilated(u, conv_w[:, None, :], window_strides=(1,),
                                 padding=[(CONV_WIDTH - 1, 0)],
                                 dimension_numbers=('NWC', 'WIO', 'NWC'),
                                 feature_group_count=C) + conv_b
    val, gate = jnp.split(u, 2, axis=-1)
    return (jax.nn.silu(gate) * val) @ w_down


def setup_inputs(seed: int = 0) -> dict:
    key = jax.random.key(seed)
    ks = jax.random.split(key, 16)
    D, F, G = D_MODEL, D_FF, POOL_GROUP
    nrm = jax.random.normal
    return {
        'x': nrm(ks[0], (BATCH, SEQ, D), jnp.float32),
        'c': nrm(ks[1], (BATCH, D), jnp.float32),
        'norm_g': 1.0 + 0.05 * nrm(ks[2], (DEPTH, 2, D), jnp.float32),
        'w_ada': 0.5 * D ** -0.5 * nrm(ks[3], (DEPTH, D, 6 * D), jnp.float32),
        'b_ada': 0.02 * nrm(ks[4], (DEPTH, 6 * D), jnp.float32),
        'pool_w': G ** -0.5 * nrm(ks[5], (N_POOL_LAYERS, N_POOL_GROUPS, G, G), jnp.float32),
        'pool_scale': 1.0 + 0.1 * nrm(ks[6], (N_POOL_LAYERS, D), jnp.float32),
        'w_qkv': D ** -0.5 * nrm(ks[7], (N_MOBA_LAYERS, D, 3 * D), jnp.float32),
        'w_o': D ** -0.5 * nrm(ks[8], (N_MOBA_LAYERS, D, D), jnp.float32),
        'rel_table': 0.5 * nrm(ks[9], (REL_BUCKETS, N_HEADS), jnp.float32),
        'w_up': D ** -0.5 * nrm(ks[10], (DEPTH, D, 2 * F), jnp.float32),
        'conv_w': CONV_WIDTH ** -0.5 * nrm(ks[11], (DEPTH, CONV_WIDTH, 2 * F), jnp.float32),
        'conv_b': 0.02 * nrm(ks[12], (DEPTH, 2 * F), jnp.float32),
        'w_down': F ** -0.5 * nrm(ks[13], (DEPTH, F, D), jnp.float32),
        'final_g': 1.0 + 0.05 * nrm(ks[14], (D,), jnp.float32),
    }


def reference(x, c, norm_g, w_ada, b_ada, pool_w, pool_scale, w_qkv, w_o, rel_table,
              w_up, conv_w, conv_b, w_down, final_g):
    cond = jax.nn.silu(c)
    for i in range(DEPTH):
        mod = cond @ w_ada[i] + b_ada[i]
        sh1, sc1, g1, sh2, sc2, g2 = jnp.split(mod, 6, axis=-1)
        h = modulate(rmsnorm(x, norm_g[i, 0]), sh1, sc1)
        li = i // N_MIXERS
        if i % N_MIXERS == 0:
            y = pool_mixer(h, pool_w[li], pool_scale[li])
        else:
            y = moba_attention(h, w_qkv[li], w_o[li], rel_table)
        x = x + g1[:, None, :] * y
        h = modulate(rmsnorm(x, norm_g[i, 1]), sh2, sc2)
        x = x + g2[:, None, :] * conv_ffn(h, w_up[i], conv_w[i], conv_b[i], w_down[i])
    return rmsnorm(x, final_g)
```

```python
import functools

import numpy as np
import jax
import jax.numpy as jnp
from jax import lax
from jax.experimental import pallas as pl
from jax.experimental.pallas import tpu as pltpu

F32 = jnp.float32
BF16 = jnp.bfloat16

N_MIXERS = 2
POOL_WINDOWS = (2, 4, 8, 16)
HEAD_DIM = 128
MOBA_BLOCK = 256
MOBA_TOPK = 3
REL_BUCKETS = 32
REL_MAX_DIST = 128
CONV_WIDTH = 3
EPS = 1e-6
NEG = -1e30
N_MOD = 6
MOD_ROWS = 8
POOL_HALO = 16
VMEM_LIMIT = 56 * 1024 * 1024


def _cparams(n_axes):
    return pltpu.CompilerParams(dimension_semantics=("arbitrary",) * n_axes,
                                vmem_limit_bytes=VMEM_LIMIT)


def _silu(v):
    return v / (1.0 + jnp.exp(-v))


def _rms(xv):
    return xv * lax.rsqrt(jnp.mean(xv * xv, axis=-1, keepdims=True) + EPS)


def _ada_kernel(c_ref, w_ref, b_ref, o_ref):
    cond = _silu(c_ref[...]).astype(BF16)
    o_ref[0] = jnp.dot(cond, w_ref[0].astype(BF16), preferred_element_type=F32) + b_ref[0]


def _ada(c, w_ada, b_ada):
    depth, d, _ = w_ada.shape
    cp = jnp.pad(c, ((0, MOD_ROWS - c.shape[0]), (0, 0)))
    b3 = b_ada.reshape(depth * N_MOD, 1, d)
    return pl.pallas_call(
        _ada_kernel,
        grid=(depth, N_MOD),
        in_specs=[pl.BlockSpec((MOD_ROWS, d), lambda i, k: (0, 0)),
                  pl.BlockSpec((1, d, d), lambda i, k: (i, 0, k)),
                  pl.BlockSpec((1, 1, d), lambda i, k: (i * N_MOD + k, 0, 0))],
        out_specs=pl.BlockSpec((1, MOD_ROWS, d), lambda i, k: (i * N_MOD + k, 0, 0)),
        out_shape=jax.ShapeDtypeStruct((depth * N_MOD, MOD_ROWS, d), F32),
        compiler_params=_cparams(2),
        name="ada",
    )(cp, w_ada, b3)


def _mod_spec(layer, k, d):
    idx = layer * N_MOD + k
    return pl.BlockSpec((1, MOD_ROWS, d), lambda *_: (idx, 0, 0))


def _row_spec(idx, d):
    return pl.BlockSpec((1, 1, d), lambda *_: (idx, 0, 0))


def _rel_bucket_np(dist):
    n = np.maximum(dist, 0)
    max_exact = REL_BUCKETS // 2
    nf = np.maximum(n, 1).astype(np.float32)
    large = max_exact + (np.log(nf / np.float32(max_exact)) / np.float32(np.log(REL_MAX_DIST / max_exact))
                         * np.float32(REL_BUCKETS - max_exact)).astype(np.int32)
    large = np.minimum(large, REL_BUCKETS - 1)
    return np.where(n < max_exact, n, large).astype(np.int32)


def _bucket_tiles():
    t = np.arange(MOBA_BLOCK)[:, None]
    s = np.arange(MOBA_BLOCK)[None, :]
    prev = _rel_bucket_np(MOBA_BLOCK + t - s)
    own = np.where(t - s >= 0, _rel_bucket_np(t - s), -1)
    assert _rel_bucket_np(np.arange(MOBA_BLOCK + 1, 16 * MOBA_BLOCK)).min() == REL_BUCKETS - 1
    return np.stack([prev, own]).astype(np.int32)


def _relbias_kernel(tab_ref, bkt_ref, o_ref):
    h = pl.program_id(0)
    for t in range(2):
        bk = bkt_ref[t]
        acc = jnp.full(bk.shape, NEG, F32)
        for b in range(REL_BUCKETS):
            acc = jnp.where(bk == b, tab_ref[b, h], acc)
        o_ref[0, t] = acc


def _relbias(rel_table):
    n_heads = rel_table.shape[1]
    bkt = jnp.asarray(_bucket_tiles())
    blk = MOBA_BLOCK
    return pl.pallas_call(
        _relbias_kernel,
        grid=(n_heads,),
        in_specs=[pl.BlockSpec(memory_space=pltpu.SMEM),
                  pl.BlockSpec((2, blk, blk), lambda h: (0, 0, 0))],
        out_specs=pl.BlockSpec((1, 2, blk, blk), lambda h: (h, 0, 0, 0)),
        out_shape=jax.ShapeDtypeStruct((n_heads, 2, blk, blk), F32),
        compiler_params=_cparams(1),
        name="relbias",
    )(rel_table, bkt)


def _pool_kernel(x_ref, xh_ref, ng1_ref, sh1_ref, sc1_ref, gt1_ref, w_ref, ps_ref,
                 ng2_ref, sh2_ref, sc2_ref, xo_ref, ho_ref, hs_ref, *, tm, seq):
    i = pl.program_id(0)
    tiles_per_seq = seq // tm
    b = lax.div(i, tiles_per_seq)
    row0 = lax.rem(i, tiles_per_seq) * tm
    group = xo_ref.shape[1] // len(POOL_WINDOWS)

    def row(ref):
        return ref[0, pl.ds(b, 1), :]

    def normmod(xv, g, shift, scale):
        return (_rms(xv) * g) * (1.0 + scale) + shift

    g1, shift1, scale1 = ng1_ref[0], row(sh1_ref), row(sc1_ref)
    hs_ref[POOL_HALO:, :] = normmod(x_ref[...], g1, shift1, scale1)
    halo = normmod(xh_ref[...], g1, shift1, scale1)
    hs_ref[0:POOL_HALO, :] = jnp.where(row0 == 0, 0.0, halo)

    pos = row0 + lax.broadcasted_iota(jnp.int32, (tm, 1), 0)
    gate1 = row(gt1_ref)
    for g, win in enumerate(POOL_WINDOWS):
        cols = slice(g * group, (g + 1) * group)
        h0 = hs_ref[POOL_HALO:POOL_HALO + tm, cols]
        acc = h0
        for j in range(1, win):
            acc = acc + hs_ref[POOL_HALO - j:POOL_HALO - j + tm, cols]
        cnt = jnp.minimum(pos + 1, win).astype(F32)
        dlt = (acc / cnt - h0).astype(BF16)
        y = jnp.dot(dlt, w_ref[g], preferred_element_type=F32)
        xo_ref[:, cols] = x_ref[:, cols] + gate1[:, cols] * (y * ps_ref[0][:, cols])

    ho_ref[...] = normmod(xo_ref[...], ng2_ref[0], row(sh2_ref), row(sc2_ref)).astype(BF16)


def _pool(xf, mod, norm_g3, pool_w_b, pool_scale3, layer, li, seq):
    m, d = xf.shape
    tm = 512
    n_grp, grp, _ = pool_w_b.shape[1:]
    halo_blocks = tm // POOL_HALO
    kern = functools.partial(_pool_kernel, tm=tm, seq=seq)
    return pl.pallas_call(
        kern,
        grid=(m // tm,),
        in_specs=[pl.BlockSpec((tm, d), lambda i: (i, 0)),
                  pl.BlockSpec((POOL_HALO, d), lambda i: (jnp.maximum(i * halo_blocks - 1, 0), 0)),
                  _row_spec(layer * 2, d),
                  _mod_spec(layer, 0, d), _mod_spec(layer, 1, d), _mod_spec(layer, 2, d),
                  pl.BlockSpec((None, n_grp, grp, grp), lambda i: (li, 0, 0, 0)),
                  _row_spec(li, d),
                  _row_spec(layer * 2 + 1, d),
                  _mod_spec(layer, 3, d), _mod_spec(layer, 4, d)],
        out_specs=[pl.BlockSpec((tm, d), lambda i: (i, 0)),
                   pl.BlockSpec((tm, d), lambda i: (i, 0))],
        out_shape=[jax.ShapeDtypeStruct((m, d), F32), jax.ShapeDtypeStruct((m, d), BF16)],
        scratch_shapes=[pltpu.VMEM((tm + POOL_HALO, d), F32)],
        compiler_params=_cparams(1),
        name=f"pool{layer}",
    )(xf, xf, norm_g3, mod, mod, mod, pool_w_b, pool_scale3, norm_g3, mod, mod)


def _mm_res_kernel(*refs, mode, tm, seq):
    if mode == "x":
        a_ref, w_ref, x_ref, gt_ref, xo_ref = refs
    elif mode == "xh":
        a_ref, w_ref, x_ref, gt_ref, ng_ref, sh_ref, sc_ref, xo_ref, ho_ref = refs
    else:
        a_ref, w_ref, x_ref, gt_ref, ng_ref, fo_ref = refs
    b = lax.div(pl.program_id(0), seq // tm)
    y = jnp.dot(a_ref[...], w_ref[...], preferred_element_type=F32)
    xn = x_ref[...] + gt_ref[0, pl.ds(b, 1), :] * y
    if mode in ("x", "xh"):
        xo_ref[...] = xn
    if mode == "xh":
        shift, scale = sh_ref[0, pl.ds(b, 1), :], sc_ref[0, pl.ds(b, 1), :]
        ho_ref[...] = ((_rms(xn) * ng_ref[0]) * (1.0 + scale) + shift).astype(BF16)
    if mode == "final":
        fo_ref[...] = _rms(xn) * ng_ref[0]


def _mm_res(a, w_b, wi, xf, mod, gate_at, seq, *, mode, norm=None, norm_idx=None, next_layer=None):
    m, kdim = a.shape
    d = xf.shape[1]
    tm = 256
    kern = functools.partial(_mm_res_kernel, mode=mode, tm=tm, seq=seq)
    in_specs = [pl.BlockSpec((tm, kdim), lambda i: (i, 0)),
                pl.BlockSpec((None, kdim, d), lambda i: (wi, 0, 0), pipeline_mode=pl.Buffered(1)),
                pl.BlockSpec((tm, d), lambda i: (i, 0)),
                _mod_spec(gate_at[0], gate_at[1], d)]
    args = [a, w_b, xf, mod]
    row_out = pl.BlockSpec((tm, d), lambda i: (i, 0))
    if mode == "x":
        out_specs, out_shape = [row_out], [jax.ShapeDtypeStruct((m, d), F32)]
    elif mode == "xh":
        in_specs += [_row_spec(norm_idx, d), _mod_spec(next_layer[0], next_layer[1], d),
                     _mod_spec(next_layer[0], next_layer[1] + 1, d)]
        args += [norm, mod, mod]
        out_specs = [row_out, row_out]
        out_shape = [jax.ShapeDtypeStruct((m, d), F32), jax.ShapeDtypeStruct((m, d), BF16)]
    else:
        in_specs += [_row_spec(norm_idx, d)]
        args += [norm]
        out_specs, out_shape = [row_out], [jax.ShapeDtypeStruct((m, d), F32)]
    return pl.pallas_call(
        kern, grid=(m // tm,), in_specs=in_specs, out_specs=out_specs, out_shape=out_shape,
        compiler_params=_cparams(1), name=f"mmres_{mode}_{kdim}",
    )(*args)


def _up_kernel(a_ref, wv_ref, wg_ref, cwv_ref, cwg_ref, bv_ref, bg_ref, o_ref, uv_ref, ug_ref, *, rc):
    seq = a_ref.shape[0]
    pad = uv_ref.shape[0] - seq
    uv_ref[0:pad, :] = jnp.zeros((pad, uv_ref.shape[1]), F32)
    ug_ref[0:pad, :] = jnp.zeros((pad, ug_ref.shape[1]), F32)

    def conv(u_ref, cw_ref, b_ref, r0):
        acc = b_ref[0]
        for j in range(CONV_WIDTH):
            off = pad + r0 - (CONV_WIDTH - 1 - j)
            acc = acc + cw_ref[0, j:j + 1, :] * u_ref[off:off + rc, :]
        return acc

    for c in range(seq // rc):
        r0 = c * rc
        a = a_ref[r0:r0 + rc, :]
        uv_ref[pad + r0:pad + r0 + rc, :] = jnp.dot(a, wv_ref[...], preferred_element_type=F32)
        ug_ref[pad + r0:pad + r0 + rc, :] = jnp.dot(a, wg_ref[...], preferred_element_type=F32)
        val = conv(uv_ref, cwv_ref, bv_ref, r0)
        gate = conv(ug_ref, cwg_ref, bg_ref, r0)
        o_ref[r0:r0 + rc, :] = (_silu(gate) * val).astype(BF16)


def _up(h, w_up_b, conv_w, conv_b3, layer, seq):
    m, d = h.shape
    f = w_up_b.shape[2] // 2
    tf = 512
    nf = f // tf
    pad = 8
    kern = functools.partial(_up_kernel, rc=256)
    return pl.pallas_call(
        kern,
        grid=(m // seq, nf),
        in_specs=[pl.BlockSpec((seq, d), lambda b, j: (b, 0)),
                  pl.BlockSpec((None, d, tf), lambda b, j: (layer, 0, j)),
                  pl.BlockSpec((None, d, tf), lambda b, j: (layer, 0, nf + j)),
                  pl.BlockSpec((1, CONV_WIDTH, tf), lambda b, j: (layer, 0, j)),
                  pl.BlockSpec((1, CONV_WIDTH, tf), lambda b, j: (layer, 0, nf + j)),
                  pl.BlockSpec((1, 1, tf), lambda b, j: (layer, 0, j)),
                  pl.BlockSpec((1, 1, tf), lambda b, j: (layer, 0, nf + j))],
        out_specs=pl.BlockSpec((seq, tf), lambda b, j: (b, j)),
        out_shape=jax.ShapeDtypeStruct((m, f), BF16),
        scratch_shapes=[pltpu.VMEM((pad + seq, tf), F32), pltpu.VMEM((pad + seq, tf), F32)],
        compiler_params=_cparams(2),
        name=f"up{layer}",
    )(h, w_up_b, w_up_b, conv_w, conv_w, conv_b3, conv_b3)


def _qkv_kernel(a_ref, w_ref, o_ref, *, d_model):
    y = jnp.dot(a_ref[...], w_ref[...], preferred_element_type=F32)
    tn = o_ref.shape[1]
    scale = jnp.where(pl.program_id(1) * tn < d_model, HEAD_DIM ** -0.5, 1.0)
    o_ref[...] = (y * scale).astype(BF16)


def _qkv(h, w_qkv_b, li):
    m, d = h.shape
    n = w_qkv_b.shape[2]
    tm, tn = 1024, 1024
    assert d % tn == 0
    return pl.pallas_call(
        functools.partial(_qkv_kernel, d_model=d),
        grid=(m // tm, n // tn),
        in_specs=[pl.BlockSpec((tm, d), lambda i, j: (i, 0)),
                  pl.BlockSpec((None, d, tn), lambda i, j: (li, 0, j))],
        out_specs=pl.BlockSpec((tm, tn), lambda i, j: (i, j)),
        out_shape=jax.ShapeDtypeStruct((m, n), BF16),
        compiler_params=_cparams(2),
        name="qkv",
    )(h, w_qkv_b)


def _attn_kernel(tab_ref, q_ref, k_ref, v_ref, bias_ref, o_ref):
    head = pl.program_id(1)
    blk = MOBA_BLOCK
    nb = q_ref.shape[0] // blk
    far = tab_ref[REL_BUCKETS - 1, head]
    nt = (((1,), (1,)), ((), ()))
    for j in range(nb):
        span = (j + 1) * blk
        qj = q_ref[j * blk:(j + 1) * blk, :]
        s = lax.dot_general(qj, k_ref[0:span, :], nt, preferred_element_type=F32)
        sb = [s[:, n * blk:(n + 1) * blk] for n in range(j + 1)]
        allowed = None
        if j > MOBA_TOPK:
            gates = [jnp.sum(sb[n], axis=-1, keepdims=True) * (1.0 / blk) for n in range(j)]
            rank = [jnp.zeros((blk, 1), jnp.int32) for _ in range(j)]
            for n in range(j):
                for mm in range(n):
                    m_first = gates[mm] >= gates[n]
                    rank[n] = rank[n] + m_first.astype(jnp.int32)
                    rank[mm] = rank[mm] + (1 - m_first.astype(jnp.int32))
            allowed = [r < MOBA_TOPK for r in rank]
        logits = []
        for n in range(j + 1):
            if n == j:
                lg = sb[n] + bias_ref[0, 1]
            elif n == j - 1:
                lg = sb[n] + bias_ref[0, 0]
            else:
                lg = sb[n] + far
            if allowed is not None and n < j:
                lg = jnp.where(allowed[n], lg, NEG)
            logits.append(lg)
        mx = jnp.max(functools.reduce(jnp.maximum, logits), axis=-1, keepdims=True)
        ps = [jnp.exp(lg - mx) for lg in logits]
        denom = jnp.sum(functools.reduce(jnp.add, ps), axis=-1, keepdims=True)
        p = jnp.concatenate([pp.astype(BF16) for pp in ps], axis=-1)
        o = jnp.dot(p, v_ref[0:span, :], preferred_element_type=F32) / denom
        o_ref[j * blk:(j + 1) * blk, :] = o.astype(BF16)


def _attn(qkv, bias, rel_table, seq):
    m, n3 = qkv.shape
    d = n3 // 3
    n_heads = d // HEAD_DIM
    blk = MOBA_BLOCK
    assert seq % blk == 0
    return pl.pallas_call(
        _attn_kernel,
        grid=(m // seq, n_heads),
        in_specs=[pl.BlockSpec(memory_space=pltpu.SMEM),
                  pl.BlockSpec((seq, HEAD_DIM), lambda b, h: (b, h)),
                  pl.BlockSpec((seq, HEAD_DIM), lambda b, h: (b, n_heads + h)),
                  pl.BlockSpec((seq, HEAD_DIM), lambda b, h: (b, 2 * n_heads + h)),
                  pl.BlockSpec((1, 2, blk, blk), lambda b, h: (h, 0, 0, 0))],
        out_specs=pl.BlockSpec((seq, HEAD_DIM), lambda b, h: (b, h)),
        out_shape=jax.ShapeDtypeStruct((m, d), BF16),
        compiler_params=_cparams(2),
        name="attn",
    )(rel_table, qkv, qkv, qkv, bias)


def kernel(x, c, norm_g, w_ada, b_ada, pool_w, pool_scale, w_qkv, w_o, rel_table,
           w_up, conv_w, conv_b, w_down, final_g):
    bsz, seq, d = x.shape
    depth = w_ada.shape[0]
    m = bsz * seq

    mod = _ada(c, w_ada, b_ada)
    bias = _relbias(rel_table)

    pool_w_b, w_qkv_b, w_o_b = pool_w.astype(BF16), w_qkv.astype(BF16), w_o.astype(BF16)
    w_up_b, w_down_b = w_up.astype(BF16), w_down.astype(BF16)
    norm_g3 = norm_g.reshape(depth * 2, 1, d)
    pool_scale3 = pool_scale.reshape(-1, 1, d)
    conv_b3 = conv_b.reshape(depth, 1, -1)
    final_g3 = final_g.reshape(1, 1, d)

    xf = x.reshape(m, d)
    h = None
    out = None
    for i in range(depth):
        li = i // N_MIXERS
        if i % N_MIXERS == 0:
            xf, h2 = _pool(xf, mod, norm_g3, pool_w_b, pool_scale3, i, li, seq)
        else:
            o = _attn(_qkv(h, w_qkv_b, li), bias, rel_table, seq)
            xf, h2 = _mm_res(o, w_o_b, li, xf, mod, (i, 2), seq, mode="xh",
                             norm=norm_g3, norm_idx=i * 2 + 1, next_layer=(i, 3))
        a = _up(h2, w_up_b, conv_w, conv_b3, i, seq)
        if i == depth - 1:
            (out,) = _mm_res(a, w_down_b, i, xf, mod, (i, 5), seq, mode="final",
                             norm=final_g3, norm_idx=0)
        elif (i + 1) % N_MIXERS == 0:
            (xf,) = _mm_res(a, w_down_b, i, xf, mod, (i, 5), seq, mode="x")
        else:
            xf, h = _mm_res(a, w_down_b, i, xf, mod, (i, 5), seq, mode="xh",
                            norm=norm_g3, norm_idx=(i + 1) * 2, next_layer=(i + 1, 0))
    return out.reshape(bsz, seq, d)
```

```python
import functools

import numpy as np
import jax
import jax.numpy as jnp
from jax import lax
from jax.experimental import pallas as pl
from jax.experimental.pallas import tpu as pltpu

F32 = jnp.float32
BF16 = jnp.bfloat16

N_MIXERS = 2
POOL_WINDOWS = (2, 4, 8, 16)
HEAD_DIM = 128
MOBA_BLOCK = 256
MOBA_TOPK = 3
REL_BUCKETS = 32
REL_MAX_DIST = 128
CONV_WIDTH = 3
EPS = 1e-6
NEG = -1e30
N_MOD = 6
SUBLANES = 8
MOD_ROWS = SUBLANES
POOL_HALO = 16
CAST_ROWS = 256
VMEM_LIMIT = 56 * 1024 * 1024


def _cparams(n_axes):
    return pltpu.CompilerParams(dimension_semantics=("arbitrary",) * n_axes,
                                vmem_limit_bytes=VMEM_LIMIT)


def _silu(v):
    return v / (1.0 + jnp.exp(-v))


def _rms(xv):
    return xv * lax.rsqrt(jnp.mean(xv * xv, axis=-1, keepdims=True) + EPS)


def _cast_rows(pred, pairs):
    n_rows = pairs[0][0].shape[0]
    assert all(src.shape[0] == n_rows for src, _ in pairs) and n_rows % CAST_ROWS == 0

    def body(r, carry):
        rows = pl.ds(pl.multiple_of(r * CAST_ROWS, CAST_ROWS), CAST_ROWS)
        for src, dst in pairs:
            dst[rows, :] = src[rows, :].astype(BF16)
        return carry

    lax.fori_loop(0, jnp.where(pred, n_rows // CAST_ROWS, 0), body, 0)


def _ada_kernel(c_ref, w_ref, b_ref, o_ref):
    cond = _silu(c_ref[...]).astype(BF16)
    o_ref[0] = jnp.dot(cond, w_ref[0].astype(BF16), preferred_element_type=F32) + b_ref[0]


def _ada(c, w_ada, b_ada):
    depth, d, _ = w_ada.shape
    cp = jnp.pad(c, ((0, MOD_ROWS - c.shape[0]), (0, 0)))
    b3 = b_ada.reshape(depth * N_MOD, 1, d)
    return pl.pallas_call(
        _ada_kernel,
        grid=(depth, N_MOD),
        in_specs=[pl.BlockSpec((MOD_ROWS, d), lambda i, k: (0, 0)),
                  pl.BlockSpec((1, d, d), lambda i, k: (i, 0, k)),
                  pl.BlockSpec((1, 1, d), lambda i, k: (i * N_MOD + k, 0, 0))],
        out_specs=pl.BlockSpec((1, MOD_ROWS, d), lambda i, k: (i * N_MOD + k, 0, 0)),
        out_shape=jax.ShapeDtypeStruct((depth * N_MOD, MOD_ROWS, d), F32),
        compiler_params=_cparams(2),
        name="ada",
    )(cp, w_ada, b3)


def _mod_spec(layer, k, d):
    idx = layer * N_MOD + k
    return pl.BlockSpec((1, MOD_ROWS, d), lambda *_: (idx, 0, 0))


def _row_spec(idx, d):
    return pl.BlockSpec((1, 1, d), lambda *_: (idx, 0, 0))


def _rel_bucket_np(dist):
    n = np.maximum(dist, 0)
    max_exact = REL_BUCKETS // 2
    nf = np.maximum(n, 1).astype(np.float32)
    large = max_exact + (np.log(nf / np.float32(max_exact)) / np.float32(np.log(REL_MAX_DIST / max_exact))
                         * np.float32(REL_BUCKETS - max_exact)).astype(np.int32)
    large = np.minimum(large, REL_BUCKETS - 1)
    return np.where(n < max_exact, n, large).astype(np.int32)


def _bucket_tiles():
    t = np.arange(MOBA_BLOCK)[:, None]
    s = np.arange(MOBA_BLOCK)[None, :]
    prev = _rel_bucket_np(MOBA_BLOCK + t - s)
    own = np.where(t - s >= 0, _rel_bucket_np(t - s), -1)
    assert _rel_bucket_np(np.arange(MOBA_BLOCK + 1, 16 * MOBA_BLOCK)).min() == REL_BUCKETS - 1
    return np.stack([prev, own]).astype(np.int32)


def _relbias_kernel(tab_ref, bkt_ref, o_ref):
    h = pl.program_id(0)
    for t in range(2):
        bk = bkt_ref[t]
        acc = jnp.full(bk.shape, NEG, F32)
        for b in range(REL_BUCKETS):
            acc = jnp.where(bk == b, tab_ref[b, h], acc)
        o_ref[0, t] = acc


def _relbias(rel_table):
    n_heads = rel_table.shape[1]
    bkt = jnp.asarray(_bucket_tiles())
    blk = MOBA_BLOCK
    return pl.pallas_call(
        _relbias_kernel,
        grid=(n_heads,),
        in_specs=[pl.BlockSpec(memory_space=pltpu.SMEM),
                  pl.BlockSpec((2, blk, blk), lambda h: (0, 0, 0))],
        out_specs=pl.BlockSpec((1, 2, blk, blk), lambda h: (h, 0, 0, 0)),
        out_shape=jax.ShapeDtypeStruct((n_heads, 2, blk, blk), F32),
        compiler_params=_cparams(1),
        name="relbias",
    )(rel_table, bkt)


def _pool_kernel(x_ref, xh_ref, ng1_ref, sh1_ref, sc1_ref, gt1_ref, w_ref, ps_ref,
                 ng2_ref, sh2_ref, sc2_ref, xo_ref, ho_ref, hs_ref, *, tm, seq):
    i = pl.program_id(0)
    tiles_per_seq = seq // tm
    b = lax.div(i, tiles_per_seq)
    row0 = lax.rem(i, tiles_per_seq) * tm
    group = xo_ref.shape[1] // len(POOL_WINDOWS)

    def row(ref):
        return ref[0, pl.ds(b, 1), :]

    def normmod(xv, g, shift, scale):
        return (_rms(xv) * g) * (1.0 + scale) + shift

    g1, shift1, scale1 = ng1_ref[0], row(sh1_ref), row(sc1_ref)
    hs_ref[POOL_HALO:, :] = normmod(x_ref[...], g1, shift1, scale1)
    halo = normmod(xh_ref[...], g1, shift1, scale1)
    hs_ref[0:POOL_HALO, :] = jnp.where(row0 == 0, 0.0, halo)

    pos = row0 + lax.broadcasted_iota(jnp.int32, (tm, 1), 0)
    gate1 = row(gt1_ref)
    for g, win in enumerate(POOL_WINDOWS):
        cols = slice(g * group, (g + 1) * group)
        h0 = hs_ref[POOL_HALO:POOL_HALO + tm, cols]
        acc = h0
        for j in range(1, win):
            acc = acc + hs_ref[POOL_HALO - j:POOL_HALO - j + tm, cols]
        cnt = jnp.minimum(pos + 1, win).astype(F32)
        dlt = (acc / cnt - h0).astype(BF16)
        y = jnp.dot(dlt, w_ref[g], preferred_element_type=F32)
        xo_ref[:, cols] = x_ref[:, cols] + gate1[:, cols] * (y * ps_ref[0][:, cols])

    ho_ref[...] = normmod(xo_ref[...], ng2_ref[0], row(sh2_ref), row(sc2_ref)).astype(BF16)


def _pool(xf, mod, norm_g3, pool_w_b, pool_scale3, layer, li, seq):
    m, d = xf.shape
    tm = 512
    n_grp, grp, _ = pool_w_b.shape[1:]
    halo_blocks = tm // POOL_HALO
    kern = functools.partial(_pool_kernel, tm=tm, seq=seq)
    return pl.pallas_call(
        kern,
        grid=(m // tm,),
        in_specs=[pl.BlockSpec((tm, d), lambda i: (i, 0)),
                  pl.BlockSpec((POOL_HALO, d), lambda i: (jnp.maximum(i * halo_blocks - 1, 0), 0)),
                  _row_spec(layer * 2, d),
                  _mod_spec(layer, 0, d), _mod_spec(layer, 1, d), _mod_spec(layer, 2, d),
                  pl.BlockSpec((None, n_grp, grp, grp), lambda i: (li, 0, 0, 0)),
                  _row_spec(li, d),
                  _row_spec(layer * 2 + 1, d),
                  _mod_spec(layer, 3, d), _mod_spec(layer, 4, d)],
        out_specs=[pl.BlockSpec((tm, d), lambda i: (i, 0)),
                   pl.BlockSpec((tm, d), lambda i: (i, 0))],
        out_shape=[jax.ShapeDtypeStruct((m, d), F32), jax.ShapeDtypeStruct((m, d), BF16)],
        scratch_shapes=[pltpu.VMEM((tm + POOL_HALO, d), F32)],
        compiler_params=_cparams(1),
        name=f"pool{layer}",
    )(xf, xf, norm_g3, mod, mod, mod, pool_w_b, pool_scale3, norm_g3, mod, mod)


def _mm_res_kernel(*refs, mode, tm, seq, cast):
    if cast:
        *refs, wb_ref = refs
    if mode == "x":
        a_ref, w_ref, x_ref, gt_ref, xo_ref = refs
    elif mode == "xh":
        a_ref, w_ref, x_ref, gt_ref, ng_ref, sh_ref, sc_ref, xo_ref, ho_ref = refs
    else:
        a_ref, w_ref, x_ref, gt_ref, ng_ref, fo_ref = refs
    if cast:
        _cast_rows(pl.program_id(0) == 0, [(w_ref, wb_ref)])
        w_ref = wb_ref
    b = lax.div(pl.program_id(0), seq // tm)
    y = jnp.dot(a_ref[...], w_ref[...], preferred_element_type=F32)
    xn = x_ref[...] + gt_ref[0, pl.ds(b, 1), :] * y
    if mode in ("x", "xh"):
        xo_ref[...] = xn
    if mode == "xh":
        shift, scale = sh_ref[0, pl.ds(b, 1), :], sc_ref[0, pl.ds(b, 1), :]
        ho_ref[...] = ((_rms(xn) * ng_ref[0]) * (1.0 + scale) + shift).astype(BF16)
    if mode == "final":
        fo_ref[...] = _rms(xn) * ng_ref[0]


def _mm_res(a, w_b, wi, xf, mod, gate_at, seq, *, mode, norm=None, norm_idx=None, next_layer=None):
    m, kdim = a.shape
    d = xf.shape[1]
    tm = 256
    cast = w_b.dtype != BF16
    kern = functools.partial(_mm_res_kernel, mode=mode, tm=tm, seq=seq, cast=cast)
    in_specs = [pl.BlockSpec((tm, kdim), lambda i: (i, 0)),
                pl.BlockSpec((None, kdim, d), lambda i: (wi, 0, 0), pipeline_mode=pl.Buffered(1)),
                pl.BlockSpec((tm, d), lambda i: (i, 0)),
                _mod_spec(gate_at[0], gate_at[1], d)]
    args = [a, w_b, xf, mod]
    row_out = pl.BlockSpec((tm, d), lambda i: (i, 0))
    if mode == "x":
        out_specs, out_shape = [row_out], [jax.ShapeDtypeStruct((m, d), F32)]
    elif mode == "xh":
        in_specs += [_row_spec(norm_idx, d), _mod_spec(next_layer[0], next_layer[1], d),
                     _mod_spec(next_layer[0], next_layer[1] + 1, d)]
        args += [norm, mod, mod]
        out_specs = [row_out, row_out]
        out_shape = [jax.ShapeDtypeStruct((m, d), F32), jax.ShapeDtypeStruct((m, d), BF16)]
    else:
        in_specs += [_row_spec(norm_idx, d)]
        args += [norm]
        out_specs, out_shape = [row_out], [jax.ShapeDtypeStruct((m, d), F32)]
    return pl.pallas_call(
        kern, grid=(m // tm,), in_specs=in_specs, out_specs=out_specs, out_shape=out_shape,
        scratch_shapes=[pltpu.VMEM((kdim, d), BF16)] if cast else [],
        compiler_params=_cparams(1), name=f"mmres_{mode}_{kdim}",
    )(*args)


def _shift_rows(z, carry_row):
    rolled = pltpu.roll(z, 1, axis=0)
    first = jnp.where(lax.broadcasted_iota(jnp.int32, (SUBLANES, z.shape[1]), 0) == 0,
                      carry_row, rolled[0:SUBLANES, :])
    return jnp.concatenate([first, rolled[SUBLANES:, :]], axis=0)


def _causal_conv3(u, cw_ref, b_ref, carry):
    z0 = cw_ref[0, 0:1, :] * u
    z1 = cw_ref[0, 1:2, :] * u + _shift_rows(z0, carry[0])
    out = cw_ref[0, 2:3, :] * u + _shift_rows(z1, carry[1]) + b_ref[0]
    return out, (z0[-1:, :], z1[-1:, :])


def _up_kernel(a_ref, wv_ref, wg_ref, cwv_ref, cwg_ref, bv_ref, bg_ref, o_ref, wvb_ref, wgb_ref, *, rc):
    _cast_rows(pl.program_id(1) == 0, [(wv_ref, wvb_ref), (wg_ref, wgb_ref)])

    seq, tf = o_ref.shape
    zero = jnp.zeros((1, tf), F32)
    carry_v = carry_g = (zero, zero)

    def project(c):
        a = a_ref[c * rc:(c + 1) * rc, :]
        return (jnp.dot(a, wvb_ref[...], preferred_element_type=F32),
                jnp.dot(a, wgb_ref[...], preferred_element_type=F32))

    nxt = project(0)
    for c in range(seq // rc):
        uv, ug = nxt
        if c + 1 < seq // rc:
            nxt = project(c + 1)
        val, carry_v = _causal_conv3(uv, cwv_ref, bv_ref, carry_v)
        gate, carry_g = _causal_conv3(ug, cwg_ref, bg_ref, carry_g)
        o_ref[c * rc:(c + 1) * rc, :] = (_silu(gate) * val).astype(BF16)


def _up(h, w_up, conv_w, conv_b3, layer, seq):
    m, d = h.shape
    f = w_up.shape[2] // 2
    tf = 512
    nf = f // tf
    kern = functools.partial(_up_kernel, rc=256)
    return pl.pallas_call(
        kern,
        grid=(nf, m // seq),
        in_specs=[pl.BlockSpec((seq, d), lambda j, b: (b, 0)),
                  pl.BlockSpec((None, d, tf), lambda j, b: (layer, 0, j)),
                  pl.BlockSpec((None, d, tf), lambda j, b: (layer, 0, nf + j)),
                  pl.BlockSpec((1, CONV_WIDTH, tf), lambda j, b: (layer, 0, j)),
                  pl.BlockSpec((1, CONV_WIDTH, tf), lambda j, b: (layer, 0, nf + j)),
                  pl.BlockSpec((1, 1, tf), lambda j, b: (layer, 0, j)),
                  pl.BlockSpec((1, 1, tf), lambda j, b: (layer, 0, nf + j))],
        out_specs=pl.BlockSpec((seq, tf), lambda j, b: (b, j)),
        out_shape=jax.ShapeDtypeStruct((m, f), BF16),
        scratch_shapes=[pltpu.VMEM((d, tf), BF16), pltpu.VMEM((d, tf), BF16)],
        compiler_params=_cparams(2),
        name=f"up{layer}",
    )(h, w_up, w_up, conv_w, conv_w, conv_b3, conv_b3)


def _qkv_kernel(a_ref, w_ref, o_ref, wb_ref, *, d_model):
    _cast_rows(pl.program_id(1) == 0, [(w_ref, wb_ref)])
    y = jnp.dot(a_ref[...], wb_ref[...], preferred_element_type=F32)
    tn = o_ref.shape[1]
    scale = jnp.where(pl.program_id(0) * tn < d_model, HEAD_DIM ** -0.5, 1.0)
    o_ref[...] = (y * scale).astype(BF16)


def _qkv(h, w_qkv, li):
    m, d = h.shape
    n = w_qkv.shape[2]
    tm, tn = 1024, 1024
    assert d % tn == 0
    return pl.pallas_call(
        functools.partial(_qkv_kernel, d_model=d),
        grid=(n // tn, m // tm),
        in_specs=[pl.BlockSpec((tm, d), lambda j, i: (i, 0)),
                  pl.BlockSpec((None, d, tn), lambda j, i: (li, 0, j))],
        out_specs=pl.BlockSpec((tm, tn), lambda j, i: (i, j)),
        out_shape=jax.ShapeDtypeStruct((m, n), BF16),
        scratch_shapes=[pltpu.VMEM((d, tn), BF16)],
        compiler_params=_cparams(2),
        name="qkv",
    )(h, w_qkv)


def _attn_kernel(tab_ref, q_ref, k_ref, v_ref, bias_ref, o_ref):
    head = pl.program_id(1)
    blk = MOBA_BLOCK
    nb = q_ref.shape[0] // blk
    far = tab_ref[REL_BUCKETS - 1, head]
    nt = (((1,), (1,)), ((), ()))
    for j in range(nb):
        span = (j + 1) * blk
        qj = q_ref[j * blk:(j + 1) * blk, :]
        s = lax.dot_general(qj, k_ref[0:span, :], nt, preferred_element_type=F32)
        sb = [s[:, n * blk:(n + 1) * blk] for n in range(j + 1)]
        allowed = None
        if j > MOBA_TOPK:
            gates = [jnp.sum(sb[n], axis=-1, keepdims=True) * (1.0 / blk) for n in range(j)]
            rank = [jnp.zeros((blk, 1), jnp.int32) for _ in range(j)]
            for n in range(j):
                for mm in range(n):
                    m_first = gates[mm] >= gates[n]
                    rank[n] = rank[n] + m_first.astype(jnp.int32)
                    rank[mm] = rank[mm] + (1 - m_first.astype(jnp.int32))
            allowed = [r < MOBA_TOPK for r in rank]
        logits = []
        for n in range(j + 1):
            if n == j:
                lg = sb[n] + bias_ref[0, 1]
            elif n == j - 1:
                lg = sb[n] + bias_ref[0, 0]
            else:
                lg = sb[n] + far
            if allowed is not None and n < j:
                lg = jnp.where(allowed[n], lg, NEG)
            logits.append(lg)
        mx = jnp.max(functools.reduce(jnp.maximum, logits), axis=-1, keepdims=True)
        ps = [jnp.exp(lg - mx) for lg in logits]
        denom = jnp.sum(functools.reduce(jnp.add, ps), axis=-1, keepdims=True)
        p = jnp.concatenate([pp.astype(BF16) for pp in ps], axis=-1)
        o = jnp.dot(p, v_ref[0:span, :], preferred_element_type=F32) / denom
        o_ref[j * blk:(j + 1) * blk, :] = o.astype(BF16)


def _attn(qkv, bias, rel_table, seq):
    m, n3 = qkv.shape
    d = n3 // 3
    n_heads = d // HEAD_DIM
    blk = MOBA_BLOCK
    assert seq % blk == 0
    return pl.pallas_call(
        _attn_kernel,
        grid=(m // seq, n_heads),
        in_specs=[pl.BlockSpec(memory_space=pltpu.SMEM),
                  pl.BlockSpec((seq, HEAD_DIM), lambda b, h: (b, h)),
                  pl.BlockSpec((seq, HEAD_DIM), lambda b, h: (b, n_heads + h)),
                  pl.BlockSpec((seq, HEAD_DIM), lambda b, h: (b, 2 * n_heads + h)),
                  pl.BlockSpec((1, 2, blk, blk), lambda b, h: (h, 0, 0, 0))],
        out_specs=pl.BlockSpec((seq, HEAD_DIM), lambda b, h: (b, h)),
        out_shape=jax.ShapeDtypeStruct((m, d), BF16),
        compiler_params=_cparams(2),
        name="attn",
    )(rel_table, qkv, qkv, qkv, bias)


def kernel(x, c, norm_g, w_ada, b_ada, pool_w, pool_scale, w_qkv, w_o, rel_table,
           w_up, conv_w, conv_b, w_down, final_g):
    bsz, seq, d = x.shape
    depth = w_ada.shape[0]
    m = bsz * seq

    mod = _ada(c, w_ada, b_ada)
    bias = _relbias(rel_table)

    pool_w_b, w_down_b = pool_w.astype(BF16), w_down.astype(BF16)
    norm_g3 = norm_g.reshape(depth * 2, 1, d)
    pool_scale3 = pool_scale.reshape(-1, 1, d)
    conv_b3 = conv_b.reshape(depth, 1, -1)
    final_g3 = final_g.reshape(1, 1, d)

    xf = x.reshape(m, d)
    h = None
    out = None
    for i in range(depth):
        li = i // N_MIXERS
        if i % N_MIXERS == 0:
            xf, h2 = _pool(xf, mod, norm_g3, pool_w_b, pool_scale3, i, li, seq)
        else:
            o = _attn(_qkv(h, w_qkv, li), bias, rel_table, seq)
            xf, h2 = _mm_res(o, w_o, li, xf, mod, (i, 2), seq, mode="xh",
                             norm=norm_g3, norm_idx=i * 2 + 1, next_layer=(i, 3))
        a = _up(h2, w_up, conv_w, conv_b3, i, seq)
        if i == depth - 1:
            (out,) = _mm_res(a, w_down_b, i, xf, mod, (i, 5), seq, mode="final",
                             norm=final_g3, norm_idx=0)
        elif (i + 1) % N_MIXERS == 0:
            (xf,) = _mm_res(a, w_down_b, i, xf, mod, (i, 5), seq, mode="x")
        else:
            xf, h = _mm_res(a, w_down_b, i, xf, mod, (i, 5), seq, mode="xh",
                            norm=norm_g3, norm_idx=(i + 1) * 2, next_layer=(i + 1, 0))
    return out.reshape(bsz, seq, d)
```

```python
import functools
import math

import numpy as np
import jax
import jax.numpy as jnp
from jax import lax
from jax.experimental import pallas as pl
from jax.experimental.pallas import tpu as pltpu

F32 = jnp.float32
BF16 = jnp.bfloat16

N_MIXERS = 2
POOL_WINDOWS = (2, 4, 8, 16)
HEAD_DIM = 128
MOBA_BLOCK = 256
MOBA_TOPK = 3
REL_BUCKETS = 32
REL_MAX_DIST = 128
CONV_WIDTH = 3
EPS = 1e-6
NEG = -1e30
LOG2E = math.log2(math.e)
N_SPLIT = 3
ATTN_HEADS_PER_STEP = 2
N_MOD = 6
SUBLANES = 8
MOD_ROWS = SUBLANES
POOL_HALO = 16
CAST_ROWS = 256
VMEM_LIMIT = 56 * 1024 * 1024


def _cparams(n_axes):
    return pltpu.CompilerParams(dimension_semantics=("arbitrary",) * n_axes,
                                vmem_limit_bytes=VMEM_LIMIT)


def _silu(v):
    return v / (1.0 + jnp.exp(-v))


def _rms(xv):
    return xv * lax.rsqrt(jnp.mean(xv * xv, axis=-1, keepdims=True) + EPS)


def _cast_rows(pred, pairs):
    n_rows = pairs[0][0].shape[0]
    assert all(src.shape[0] == n_rows for src, _ in pairs) and n_rows % CAST_ROWS == 0

    def body(r, carry):
        rows = pl.ds(pl.multiple_of(r * CAST_ROWS, CAST_ROWS), CAST_ROWS)
        for src, dst in pairs:
            dst[rows, :] = src[rows, :].astype(BF16)
        return carry

    lax.fori_loop(0, jnp.where(pred, n_rows // CAST_ROWS, 0), body, 0)


def _ada_kernel(c_ref, w_ref, b_ref, o_ref):
    cond = _silu(c_ref[...]).astype(BF16)
    o_ref[0] = jnp.dot(cond, w_ref[0].astype(BF16), preferred_element_type=F32) + b_ref[0]


def _ada(c, w_ada, b_ada):
    depth, d, _ = w_ada.shape
    cp = jnp.pad(c, ((0, MOD_ROWS - c.shape[0]), (0, 0)))
    b3 = b_ada.reshape(depth * N_MOD, 1, d)
    return pl.pallas_call(
        _ada_kernel,
        grid=(depth, N_MOD),
        in_specs=[pl.BlockSpec((MOD_ROWS, d), lambda i, k: (0, 0)),
                  pl.BlockSpec((1, d, d), lambda i, k: (i, 0, k)),
                  pl.BlockSpec((1, 1, d), lambda i, k: (i * N_MOD + k, 0, 0))],
        out_specs=pl.BlockSpec((1, MOD_ROWS, d), lambda i, k: (i * N_MOD + k, 0, 0)),
        out_shape=jax.ShapeDtypeStruct((depth * N_MOD, MOD_ROWS, d), F32),
        compiler_params=_cparams(2),
        name="ada",
    )(cp, w_ada, b3)


def _mod_spec(layer, k, d):
    idx = layer * N_MOD + k
    return pl.BlockSpec((1, MOD_ROWS, d), lambda *_: (idx, 0, 0))


def _row_spec(idx, d):
    return pl.BlockSpec((1, 1, d), lambda *_: (idx, 0, 0))


def _rel_bucket_np(dist):
    n = np.maximum(dist, 0)
    max_exact = REL_BUCKETS // 2
    nf = np.maximum(n, 1).astype(np.float32)
    large = max_exact + (np.log(nf / np.float32(max_exact)) / np.float32(np.log(REL_MAX_DIST / max_exact))
                         * np.float32(REL_BUCKETS - max_exact)).astype(np.int32)
    large = np.minimum(large, REL_BUCKETS - 1)
    return np.where(n < max_exact, n, large).astype(np.int32)


def _bucket_tiles():
    t = np.arange(MOBA_BLOCK)[:, None]
    s = np.arange(MOBA_BLOCK)[None, :]
    prev = _rel_bucket_np(MOBA_BLOCK + t - s)
    own = np.where(t - s >= 0, _rel_bucket_np(t - s), -1)
    assert _rel_bucket_np(np.arange(MOBA_BLOCK + 1, 16 * MOBA_BLOCK)).min() == REL_BUCKETS - 1
    return np.stack([prev, own]).astype(np.int32)


def _relbias_kernel(tab_ref, bkt_ref, o_ref):
    h = pl.program_id(0)
    for t in range(2):
        bk = bkt_ref[t]
        acc = jnp.full(bk.shape, NEG, F32)
        for b in range(REL_BUCKETS):
            acc = jnp.where(bk == b, tab_ref[b, h] * LOG2E, acc)
        o_ref[0, t] = acc


def _relbias(rel_table):
    n_heads = rel_table.shape[1]
    bkt = jnp.asarray(_bucket_tiles())
    blk = MOBA_BLOCK
    return pl.pallas_call(
        _relbias_kernel,
        grid=(n_heads,),
        in_specs=[pl.BlockSpec(memory_space=pltpu.SMEM),
                  pl.BlockSpec((2, blk, blk), lambda h: (0, 0, 0))],
        out_specs=pl.BlockSpec((1, 2, blk, blk), lambda h: (h, 0, 0, 0)),
        out_shape=jax.ShapeDtypeStruct((n_heads, 2, blk, blk), F32),
        compiler_params=_cparams(1),
        name="relbias",
    )(rel_table, bkt)


def _pool_kernel(x_ref, xh_ref, ng1_ref, sh1_ref, sc1_ref, gt1_ref, w_ref, ps_ref,
                 ng2_ref, sh2_ref, sc2_ref, xo_ref, ho_ref, hs_ref, *, tm, seq):
    i = pl.program_id(0)
    tiles_per_seq = seq // tm
    b = lax.div(i, tiles_per_seq)
    row0 = lax.rem(i, tiles_per_seq) * tm
    group = xo_ref.shape[1] // len(POOL_WINDOWS)

    def row(ref):
        return ref[0, pl.ds(b, 1), :]

    def normmod(xv, g, shift, scale):
        return (_rms(xv) * g) * (1.0 + scale) + shift

    g1, shift1, scale1 = ng1_ref[0], row(sh1_ref), row(sc1_ref)
    hs_ref[POOL_HALO:, :] = normmod(x_ref[...], g1, shift1, scale1)
    halo = normmod(xh_ref[...], g1, shift1, scale1)
    hs_ref[0:POOL_HALO, :] = jnp.where(row0 == 0, 0.0, halo)

    pos = row0 + lax.broadcasted_iota(jnp.int32, (tm, 1), 0)
    gate1 = row(gt1_ref)
    for g, win in enumerate(POOL_WINDOWS):
        cols = slice(g * group, (g + 1) * group)
        h0 = hs_ref[POOL_HALO:POOL_HALO + tm, cols]
        acc = h0
        for j in range(1, win):
            acc = acc + hs_ref[POOL_HALO - j:POOL_HALO - j + tm, cols]
        cnt = jnp.minimum(pos + 1, win).astype(F32)
        dlt = (acc / cnt - h0).astype(BF16)
        y = jnp.dot(dlt, w_ref[g], preferred_element_type=F32)
        xo_ref[:, cols] = x_ref[:, cols] + gate1[:, cols] * (y * ps_ref[0][:, cols])

    ho_ref[...] = normmod(xo_ref[...], ng2_ref[0], row(sh2_ref), row(sc2_ref)).astype(BF16)


def _pool(xf, mod, norm_g3, pool_w_b, pool_scale3, layer, li, seq):
    m, d = xf.shape
    tm = 512
    n_grp, grp, _ = pool_w_b.shape[1:]
    halo_blocks = tm // POOL_HALO
    kern = functools.partial(_pool_kernel, tm=tm, seq=seq)
    return pl.pallas_call(
        kern,
        grid=(m // tm,),
        in_specs=[pl.BlockSpec((tm, d), lambda i: (i, 0)),
                  pl.BlockSpec((POOL_HALO, d), lambda i: (jnp.maximum(i * halo_blocks - 1, 0), 0)),
                  _row_spec(layer * 2, d),
                  _mod_spec(layer, 0, d), _mod_spec(layer, 1, d), _mod_spec(layer, 2, d),
                  pl.BlockSpec((None, n_grp, grp, grp), lambda i: (li, 0, 0, 0)),
                  _row_spec(li, d),
                  _row_spec(layer * 2 + 1, d),
                  _mod_spec(layer, 3, d), _mod_spec(layer, 4, d)],
        out_specs=[pl.BlockSpec((tm, d), lambda i: (i, 0)),
                   pl.BlockSpec((tm, d), lambda i: (i, 0))],
        out_shape=[jax.ShapeDtypeStruct((m, d), F32), jax.ShapeDtypeStruct((m, d), BF16)],
        scratch_shapes=[pltpu.VMEM((tm + POOL_HALO, d), F32)],
        compiler_params=_cparams(1),
        name=f"pool{layer}",
    )(xf, xf, norm_g3, mod, mod, mod, pool_w_b, pool_scale3, norm_g3, mod, mod)


def _mm_res_kernel(*refs, mode, tm, seq, cast):
    if cast:
        *refs, wb_ref = refs
    if mode == "x":
        a_ref, w_ref, x_ref, gt_ref, xo_ref = refs
    elif mode == "xh":
        a_ref, w_ref, x_ref, gt_ref, ng_ref, sh_ref, sc_ref, xo_ref, ho_ref = refs
    else:
        a_ref, w_ref, x_ref, gt_ref, ng_ref, fo_ref = refs
    if cast:
        _cast_rows(pl.program_id(0) == 0, [(w_ref, wb_ref)])
        w_ref = wb_ref
    b = lax.div(pl.program_id(0), seq // tm)
    y = jnp.dot(a_ref[...], w_ref[...], preferred_element_type=F32)
    xn = x_ref[...] + gt_ref[0, pl.ds(b, 1), :] * y
    if mode in ("x", "xh"):
        xo_ref[...] = xn
    if mode == "xh":
        shift, scale = sh_ref[0, pl.ds(b, 1), :], sc_ref[0, pl.ds(b, 1), :]
        ho_ref[...] = ((_rms(xn) * ng_ref[0]) * (1.0 + scale) + shift).astype(BF16)
    if mode == "final":
        fo_ref[...] = _rms(xn) * ng_ref[0]


def _mm_res(a, w_b, wi, xf, mod, gate_at, seq, *, mode, norm=None, norm_idx=None, next_layer=None):
    m, kdim = a.shape
    d = xf.shape[1]
    tm = 256
    cast = w_b.dtype != BF16
    kern = functools.partial(_mm_res_kernel, mode=mode, tm=tm, seq=seq, cast=cast)
    in_specs = [pl.BlockSpec((tm, kdim), lambda i: (i, 0)),
                pl.BlockSpec((None, kdim, d), lambda i: (wi, 0, 0), pipeline_mode=pl.Buffered(1)),
                pl.BlockSpec((tm, d), lambda i: (i, 0)),
                _mod_spec(gate_at[0], gate_at[1], d)]
    args = [a, w_b, xf, mod]
    row_out = pl.BlockSpec((tm, d), lambda i: (i, 0))
    if mode == "x":
        out_specs, out_shape = [row_out], [jax.ShapeDtypeStruct((m, d), F32)]
    elif mode == "xh":
        in_specs += [_row_spec(norm_idx, d), _mod_spec(next_layer[0], next_layer[1], d),
                     _mod_spec(next_layer[0], next_layer[1] + 1, d)]
        args += [norm, mod, mod]
        out_specs = [row_out, row_out]
        out_shape = [jax.ShapeDtypeStruct((m, d), F32), jax.ShapeDtypeStruct((m, d), BF16)]
    else:
        in_specs += [_row_spec(norm_idx, d)]
        args += [norm]
        out_specs, out_shape = [row_out], [jax.ShapeDtypeStruct((m, d), F32)]
    return pl.pallas_call(
        kern, grid=(m // tm,), in_specs=in_specs, out_specs=out_specs, out_shape=out_shape,
        scratch_shapes=[pltpu.VMEM((kdim, d), BF16)] if cast else [],
        compiler_params=_cparams(1), name=f"mmres_{mode}_{kdim}",
    )(*args)


def _shift_rows(z, carry_row):
    rolled = pltpu.roll(z, 1, axis=0)
    first = jnp.where(lax.broadcasted_iota(jnp.int32, (SUBLANES, z.shape[1]), 0) == 0,
                      carry_row, rolled[0:SUBLANES, :])
    return jnp.concatenate([first, rolled[SUBLANES:, :]], axis=0)


def _causal_conv3(u, cw_ref, b_ref, carry):
    z0 = cw_ref[0, 0:1, :] * u
    z1 = cw_ref[0, 1:2, :] * u + _shift_rows(z0, carry[0])
    out = cw_ref[0, 2:3, :] * u + _shift_rows(z1, carry[1]) + b_ref[0]
    return out, (z0[-1:, :], z1[-1:, :])


def _up_kernel(a_ref, wv_ref, wg_ref, cwv_ref, cwg_ref, bv_ref, bg_ref, wd_ref, o_ref, wdb_ref,
               wvb_ref, wgb_ref, *, rc):
    wdb_ref[...] = wd_ref[...].astype(BF16)
    _cast_rows(pl.program_id(1) == 0, [(wv_ref, wvb_ref), (wg_ref, wgb_ref)])

    seq, tf = o_ref.shape
    zero = jnp.zeros((1, tf), F32)
    carry_v = carry_g = (zero, zero)

    def project(c):
        a = a_ref[c * rc:(c + 1) * rc, :]
        return (jnp.dot(a, wvb_ref[...], preferred_element_type=F32),
                jnp.dot(a, wgb_ref[...], preferred_element_type=F32))

    nxt = project(0)
    for c in range(seq // rc):
        uv, ug = nxt
        if c + 1 < seq // rc:
            nxt = project(c + 1)
        val, carry_v = _causal_conv3(uv, cwv_ref, bv_ref, carry_v)
        gate, carry_g = _causal_conv3(ug, cwg_ref, bg_ref, carry_g)
        o_ref[c * rc:(c + 1) * rc, :] = (_silu(gate) * val).astype(BF16)


def _up(h, w_up, conv_w, conv_b3, w_down, layer, seq):
    m, d = h.shape
    f = w_up.shape[2] // 2
    tf = 512
    nf = f // tf
    nbat = m // seq
    wd_rows = f // (nf * nbat)
    assert wd_rows * nf * nbat == f and wd_rows % 16 == 0
    kern = functools.partial(_up_kernel, rc=256)
    return pl.pallas_call(
        kern,
        grid=(nf, nbat),
        in_specs=[pl.BlockSpec((seq, d), lambda j, b: (b, 0)),
                  pl.BlockSpec((None, d, tf), lambda j, b: (layer, 0, j)),
                  pl.BlockSpec((None, d, tf), lambda j, b: (layer, 0, nf + j)),
                  pl.BlockSpec((1, CONV_WIDTH, tf), lambda j, b: (layer, 0, j)),
                  pl.BlockSpec((1, CONV_WIDTH, tf), lambda j, b: (layer, 0, nf + j)),
                  pl.BlockSpec((1, 1, tf), lambda j, b: (layer, 0, j)),
                  pl.BlockSpec((1, 1, tf), lambda j, b: (layer, 0, nf + j)),
                  pl.BlockSpec((None, wd_rows, d), lambda j, b: (layer, j * nbat + b, 0))],
        out_specs=[pl.BlockSpec((seq, tf), lambda j, b: (b, j)),
                   pl.BlockSpec((None, wd_rows, d), lambda j, b: (0, j * nbat + b, 0))],
        out_shape=[jax.ShapeDtypeStruct((m, f), BF16), jax.ShapeDtypeStruct((1, f, d), BF16)],
        scratch_shapes=[pltpu.VMEM((d, tf), BF16), pltpu.VMEM((d, tf), BF16)],
        compiler_params=_cparams(2),
        name=f"up{layer}",
    )(h, w_up, w_up, conv_w, conv_w, conv_b3, conv_b3, w_down)


def _qkv_kernel(a_ref, w_ref, o_ref, wb_ref, *, d_model):
    _cast_rows(pl.program_id(1) == 0, [(w_ref, wb_ref)])
    y = jnp.dot(a_ref[...], wb_ref[...], preferred_element_type=F32)
    tn = o_ref.shape[1]
    scale = jnp.where(pl.program_id(0) * tn < d_model, HEAD_DIM ** -0.5 * LOG2E, 1.0)
    o_ref[...] = (y * scale).astype(BF16)


def _qkv(h, w_qkv, li):
    m, d = h.shape
    n = w_qkv.shape[2]
    tm, tn = 1024, 1024
    assert d % tn == 0
    return pl.pallas_call(
        functools.partial(_qkv_kernel, d_model=d),
        grid=(n // tn, m // tm),
        in_specs=[pl.BlockSpec((tm, d), lambda j, i: (i, 0)),
                  pl.BlockSpec((None, d, tn), lambda j, i: (li, 0, j))],
        out_specs=pl.BlockSpec((tm, tn), lambda j, i: (i, j)),
        out_shape=jax.ShapeDtypeStruct((m, n), BF16),
        scratch_shapes=[pltpu.VMEM((d, tn), BF16)],
        compiler_params=_cparams(2),
        name="qkv",
    )(h, w_qkv)


def _split_bf16(v):
    pieces, rest = [], v
    for _ in range(N_SPLIT):
        piece = rest.astype(BF16).astype(F32)
        pieces.append(piece)
        rest = rest - piece
    return pieces


def _key_features(seq):
    nb = seq // MOBA_BLOCK
    assert nb + N_SPLIT * nb <= HEAD_DIM
    blk_of = np.arange(seq) // MOBA_BLOCK
    kf = np.zeros((seq, HEAD_DIM), np.float32)
    for n in range(nb):
        kf[blk_of == n, n] = NEG
    for j in range(nb):
        kf[blk_of <= j - 2, nb + N_SPLIT * j:nb + N_SPLIT * (j + 1)] = 1.0
    return jnp.asarray(kf, dtype=BF16)


def _attn_head(tab_ref, head, q_ref, k_ref, v_ref, bias_ref, kf_ref, o_ref, kx_ref, vx_ref):
    blk = MOBA_BLOCK
    seq, hd = q_ref.shape
    nb = seq // blk
    nt = (((1,), (1,)), ((), ()))

    kx_ref[:, 0:hd] = k_ref[...]
    kx_ref[:, hd:] = kf_ref[...]
    vx_ref[:, 0:hd] = v_ref[...]
    vx_ref[:, hd:] = jnp.ones((seq, hd), BF16)

    lane = lax.broadcasted_iota(jnp.int32, (1, hd), 1)
    far_p = _split_bf16(jnp.full((1, hd), tab_ref[REL_BUCKETS - 1, head] * LOG2E, F32))
    kmean = jnp.sum(k_ref[...].astype(F32).reshape(nb, blk, hd), axis=1) * (1.0 / blk)
    ks = jnp.concatenate(_split_bf16(kmean) + [jnp.zeros((nb, hd), F32)], axis=0).astype(BF16)
    n_idx = lax.broadcasted_iota(jnp.int32, (nb, blk), 0)

    def scores(j):
        span = (j + 1) * blk
        qj = q_ref[j * blk:(j + 1) * blk, :]
        if j < 2:
            return lax.dot_general(qj, k_ref[0:span, :], nt, preferred_element_type=F32)
        farvec = jnp.zeros((1, hd), F32)
        for piece in range(N_SPLIT):
            farvec = jnp.where(lane == nb + N_SPLIT * j + piece, far_p[piece], farvec)
        if j > MOBA_TOPK:
            gt = lax.dot_general(ks, qj, nt, preferred_element_type=F32)
            gate = gt[0:nb] + gt[nb:2 * nb] + gt[2 * nb:3 * nb]
            rank = jnp.zeros((nb, blk), jnp.int32)
            for mm in range(j):
                gm = gate[mm:mm + 1, :]
                beats = (gm > gate) | ((gm == gate) & (mm < n_idx))
                rank = rank + beats.astype(jnp.int32)
            drop_t = ((rank >= MOBA_TOPK) & (n_idx < j)).astype(F32)
            drop = jnp.concatenate([drop_t, jnp.zeros((hd - nb, blk), F32)], axis=0).T
            qf = (drop + farvec).astype(BF16)
        else:
            qf = jnp.broadcast_to(farvec, (blk, hd)).astype(BF16)
        qx = jnp.concatenate([qj, qf], axis=1)
        return lax.dot_general(qx, kx_ref[0:span, :], nt, preferred_element_type=F32)

    order = list(reversed(range(nb)))
    s_next = scores(order[0])
    for i, j in enumerate(order):
        span = (j + 1) * blk
        s = s_next
        if i + 1 < nb:
            s_next = scores(order[i + 1])
        logits = []
        for n in range(j + 1):
            lg = s[:, n * blk:(n + 1) * blk]
            if n == j:
                lg = lg + bias_ref[1]
            elif n == j - 1:
                lg = lg + bias_ref[0]
            logits.append(lg)
        mx = jnp.max(functools.reduce(jnp.maximum, logits), axis=-1, keepdims=True)
        p = jnp.concatenate([jnp.exp2(lg - mx).astype(BF16) for lg in logits], axis=-1)
        acc = jnp.dot(p, vx_ref[0:span, :], preferred_element_type=F32)
        o_ref[j * blk:(j + 1) * blk, :] = (acc[:, 0:hd] / acc[:, hd:]).astype(BF16)


def _attn_kernel(tab_ref, q_ref, k_ref, v_ref, bias_ref, kf_ref, o_ref, kx_ref, vx_ref):
    for hh in range(ATTN_HEADS_PER_STEP):
        cols = slice(hh * HEAD_DIM, (hh + 1) * HEAD_DIM)
        _attn_head(tab_ref, pl.program_id(1) * ATTN_HEADS_PER_STEP + hh,
                   q_ref.at[:, cols], k_ref.at[:, cols], v_ref.at[:, cols], bias_ref.at[hh], kf_ref,
                   o_ref.at[:, cols], kx_ref.at[hh], vx_ref.at[hh])


def _attn(qkv, bias, rel_table, seq):
    m, n3 = qkv.shape
    d = n3 // 3
    hps = ATTN_HEADS_PER_STEP
    groups = d // (hps * HEAD_DIM)
    blk = MOBA_BLOCK
    assert seq % blk == 0
    head_cols = pl.BlockSpec((seq, hps * HEAD_DIM), lambda b, g: (b, g))
    return pl.pallas_call(
        _attn_kernel,
        grid=(m // seq, groups),
        in_specs=[pl.BlockSpec(memory_space=pltpu.SMEM),
                  head_cols,
                  pl.BlockSpec((seq, hps * HEAD_DIM), lambda b, g: (b, groups + g)),
                  pl.BlockSpec((seq, hps * HEAD_DIM), lambda b, g: (b, 2 * groups + g)),
                  pl.BlockSpec((hps, 2, blk, blk), lambda b, g: (g, 0, 0, 0)),
                  pl.BlockSpec((seq, HEAD_DIM), lambda b, g: (0, 0))],
        out_specs=head_cols,
        out_shape=jax.ShapeDtypeStruct((m, d), BF16),
        scratch_shapes=[pltpu.VMEM((hps, seq, 2 * HEAD_DIM), BF16), pltpu.VMEM((hps, seq, 2 * HEAD_DIM), BF16)],
        compiler_params=_cparams(2),
        name="attn",
    )(rel_table, qkv, qkv, qkv, bias, _key_features(seq))


def kernel(x, c, norm_g, w_ada, b_ada, pool_w, pool_scale, w_qkv, w_o, rel_table,
           w_up, conv_w, conv_b, w_down, final_g):
    bsz, seq, d = x.shape
    depth = w_ada.shape[0]
    m = bsz * seq

    mod = _ada(c, w_ada, b_ada)
    bias = _relbias(rel_table)

    pool_w_b = pool_w.astype(BF16)
    norm_g3 = norm_g.reshape(depth * 2, 1, d)
    pool_scale3 = pool_scale.reshape(-1, 1, d)
    conv_b3 = conv_b.reshape(depth, 1, -1)
    final_g3 = final_g.reshape(1, 1, d)

    xf = x.reshape(m, d)
    h = None
    out = None
    for i in range(depth):
        li = i // N_MIXERS
        if i % N_MIXERS == 0:
            xf, h2 = _pool(xf, mod, norm_g3, pool_w_b, pool_scale3, i, li, seq)
        else:
            o = _attn(_qkv(h, w_qkv, li), bias, rel_table, seq)
            xf, h2 = _mm_res(o, w_o, li, xf, mod, (i, 2), seq, mode="xh",
                             norm=norm_g3, norm_idx=i * 2 + 1, next_layer=(i, 3))
        a, w_down_b = _up(h2, w_up, conv_w, conv_b3, w_down, i, seq)
        if i == depth - 1:
            (out,) = _mm_res(a, w_down_b, 0, xf, mod, (i, 5), seq, mode="final",
                             norm=final_g3, norm_idx=0)
        elif (i + 1) % N_MIXERS == 0:
            (xf,) = _mm_res(a, w_down_b, 0, xf, mod, (i, 5), seq, mode="x")
        else:
            xf, h = _mm_res(a, w_down_b, 0, xf, mod, (i, 5), seq, mode="xh",
                            norm=norm_g3, norm_idx=(i + 1) * 2, next_layer=(i + 1, 0))
    return out.reshape(bsz, seq, d)
```

```python
import functools
import math

import numpy as np
import jax
import jax.numpy as jnp
from jax import lax
from jax.experimental import pallas as pl
from jax.experimental.pallas import tpu as pltpu

F32 = jnp.float32
BF16 = jnp.bfloat16

N_MIXERS = 2
POOL_WINDOWS = (2, 4, 8, 16)
HEAD_DIM = 128
MOBA_BLOCK = 256
MOBA_TOPK = 3
REL_BUCKETS = 32
REL_MAX_DIST = 128
CONV_WIDTH = 3
EPS = 1e-6
NEG = -1e30
LOG2E = math.log2(math.e)
N_SPLIT = 3
ATTN_HEADS_PER_STEP = 4
N_MOD = 6
SUBLANES = 8
MOD_ROWS = SUBLANES
POOL_HALO = 16
CAST_ROWS = 256
MM_ROWS = 256
VMEM_LIMIT = 56 * 1024 * 1024


def _cparams(n_axes):
    return pltpu.CompilerParams(dimension_semantics=("arbitrary",) * n_axes,
                                vmem_limit_bytes=VMEM_LIMIT)


def _silu(v):
    return v / (1.0 + jnp.exp(-v))


def _rms(xv):
    return xv * lax.rsqrt(jnp.mean(xv * xv, axis=-1, keepdims=True) + EPS)


def _cast_rows(pred, pairs):
    n_rows = pairs[0][0].shape[0]
    assert all(src.shape[0] == n_rows for src, _ in pairs) and n_rows % CAST_ROWS == 0

    def body(r, carry):
        rows = pl.ds(pl.multiple_of(r * CAST_ROWS, CAST_ROWS), CAST_ROWS)
        for src, dst in pairs:
            dst[rows, :] = src[rows, :].astype(BF16)
        return carry

    lax.fori_loop(0, jnp.where(pred, n_rows // CAST_ROWS, 0), body, 0)


def _ada_kernel(c_ref, w_ref, b_ref, o_ref):
    cond = _silu(c_ref[...]).astype(BF16)
    o_ref[0] = jnp.dot(cond, w_ref[0].astype(BF16), preferred_element_type=F32) + b_ref[0]


def _ada(c, w_ada, b_ada):
    depth, d, _ = w_ada.shape
    cp = jnp.pad(c, ((0, MOD_ROWS - c.shape[0]), (0, 0)))
    b3 = b_ada.reshape(depth * N_MOD, 1, d)
    return pl.pallas_call(
        _ada_kernel,
        grid=(depth, N_MOD),
        in_specs=[pl.BlockSpec((MOD_ROWS, d), lambda i, k: (0, 0)),
                  pl.BlockSpec((1, d, d), lambda i, k: (i, 0, k)),
                  pl.BlockSpec((1, 1, d), lambda i, k: (i * N_MOD + k, 0, 0))],
        out_specs=pl.BlockSpec((1, MOD_ROWS, d), lambda i, k: (i * N_MOD + k, 0, 0)),
        out_shape=jax.ShapeDtypeStruct((depth * N_MOD, MOD_ROWS, d), F32),
        compiler_params=_cparams(2),
        name="ada",
    )(cp, w_ada, b3)


def _mod_spec(layer, k, d):
    idx = layer * N_MOD + k
    return pl.BlockSpec((1, MOD_ROWS, d), lambda *_: (idx, 0, 0))


def _row_spec(idx, d):
    return pl.BlockSpec((1, 1, d), lambda *_: (idx, 0, 0))


def _rel_bucket_np(dist):
    n = np.maximum(dist, 0)
    max_exact = REL_BUCKETS // 2
    nf = np.maximum(n, 1).astype(np.float32)
    large = max_exact + (np.log(nf / np.float32(max_exact)) / np.float32(np.log(REL_MAX_DIST / max_exact))
                         * np.float32(REL_BUCKETS - max_exact)).astype(np.int32)
    large = np.minimum(large, REL_BUCKETS - 1)
    return np.where(n < max_exact, n, large).astype(np.int32)


def _bucket_tiles():
    t = np.arange(MOBA_BLOCK)[:, None]
    s = np.arange(MOBA_BLOCK)[None, :]
    prev = _rel_bucket_np(MOBA_BLOCK + t - s)
    own = np.where(t - s >= 0, _rel_bucket_np(t - s), -1)
    assert _rel_bucket_np(np.arange(MOBA_BLOCK + 1, 16 * MOBA_BLOCK)).min() == REL_BUCKETS - 1
    return np.stack([prev, own]).astype(np.int32)


def _relbias_kernel(tab_ref, bkt_ref, o_ref):
    h = pl.program_id(0)
    for t in range(2):
        bk = bkt_ref[t]
        acc = jnp.full(bk.shape, NEG, F32)
        for b in range(REL_BUCKETS):
            acc = jnp.where(bk == b, tab_ref[b, h] * LOG2E, acc)
        o_ref[0, t] = acc


def _relbias(rel_table):
    n_heads = rel_table.shape[1]
    bkt = jnp.asarray(_bucket_tiles())
    blk = MOBA_BLOCK
    return pl.pallas_call(
        _relbias_kernel,
        grid=(n_heads,),
        in_specs=[pl.BlockSpec(memory_space=pltpu.SMEM),
                  pl.BlockSpec((2, blk, blk), lambda h: (0, 0, 0))],
        out_specs=pl.BlockSpec((1, 2, blk, blk), lambda h: (h, 0, 0, 0)),
        out_shape=jax.ShapeDtypeStruct((n_heads, 2, blk, blk), F32),
        compiler_params=_cparams(1),
        name="relbias",
    )(rel_table, bkt)


def _pool_kernel(x_ref, xh_ref, ng1_ref, sh1_ref, sc1_ref, gt1_ref, w_ref, ps_ref,
                 ng2_ref, sh2_ref, sc2_ref, xo_ref, ho_ref, hs_ref, *, tm, seq):
    i = pl.program_id(0)
    tiles_per_seq = seq // tm
    b = lax.div(i, tiles_per_seq)
    row0 = lax.rem(i, tiles_per_seq) * tm
    group = xo_ref.shape[1] // len(POOL_WINDOWS)

    def row(ref):
        return ref[0, pl.ds(b, 1), :]

    def normmod(xv, g, shift, scale):
        return _rms(xv) * (g * (1.0 + scale)) + shift

    g1, shift1, scale1 = ng1_ref[0], row(sh1_ref), row(sc1_ref)
    hs_ref[POOL_HALO:, :] = normmod(x_ref[...], g1, shift1, scale1)
    halo = normmod(xh_ref[...], g1, shift1, scale1)
    hs_ref[0:POOL_HALO, :] = jnp.where(row0 == 0, 0.0, halo)

    pos = row0 + lax.broadcasted_iota(jnp.int32, (tm, 1), 0)
    gate1 = row(gt1_ref) * ps_ref[0]
    for g, win in enumerate(POOL_WINDOWS):
        cols = slice(g * group, (g + 1) * group)
        ext = hs_ref[:, cols]
        acc, span = ext, 1
        while span < win:
            acc = acc + pltpu.roll(acc, span, axis=0)
            span *= 2
        h0 = ext[POOL_HALO:, :]
        cnt = jnp.minimum(pos + 1, win).astype(F32)
        dlt = (acc[POOL_HALO:, :] / cnt - h0).astype(BF16)
        y = jnp.dot(dlt, w_ref[g], preferred_element_type=F32)
        xo_ref[:, cols] = x_ref[:, cols] + gate1[:, cols] * y

    ho_ref[...] = normmod(xo_ref[...], ng2_ref[0], row(sh2_ref), row(sc2_ref)).astype(BF16)


def _pool(xf, mod, norm_g3, pool_w_b, pool_scale3, layer, li, seq):
    m, d = xf.shape
    tm = 512
    n_grp, grp, _ = pool_w_b.shape[1:]
    halo_blocks = tm // POOL_HALO
    assert all(w & (w - 1) == 0 and w <= POOL_HALO for w in POOL_WINDOWS)
    kern = functools.partial(_pool_kernel, tm=tm, seq=seq)
    return pl.pallas_call(
        kern,
        grid=(m // tm,),
        in_specs=[pl.BlockSpec((tm, d), lambda i: (i, 0)),
                  pl.BlockSpec((POOL_HALO, d), lambda i: (jnp.maximum(i * halo_blocks - 1, 0), 0)),
                  _row_spec(layer * 2, d),
                  _mod_spec(layer, 0, d), _mod_spec(layer, 1, d), _mod_spec(layer, 2, d),
                  pl.BlockSpec((None, n_grp, grp, grp), lambda i: (li, 0, 0, 0)),
                  _row_spec(li, d),
                  _row_spec(layer * 2 + 1, d),
                  _mod_spec(layer, 3, d), _mod_spec(layer, 4, d)],
        out_specs=[pl.BlockSpec((tm, d), lambda i: (i, 0)),
                   pl.BlockSpec((tm, d), lambda i: (i, 0))],
        out_shape=[jax.ShapeDtypeStruct((m, d), F32), jax.ShapeDtypeStruct((m, d), BF16)],
        scratch_shapes=[pltpu.VMEM((tm + POOL_HALO, d), F32)],
        compiler_params=_cparams(1),
        name=f"pool{layer}",
    )(xf, xf, norm_g3, mod, mod, mod, pool_w_b, pool_scale3, norm_g3, mod, mod)


def _mm_res_kernel(*refs, mode, tm, seq, cast):
    if cast:
        *refs, wb_ref = refs
    if mode == "x":
        a_ref, w_ref, x_ref, gt_ref, xo_ref = refs
    elif mode == "xh":
        a_ref, w_ref, x_ref, gt_ref, ng_ref, sh_ref, sc_ref, xo_ref, ho_ref = refs
    else:
        a_ref, w_ref, x_ref, gt_ref, ng_ref, fo_ref = refs
    if cast:
        _cast_rows(pl.program_id(0) == 0, [(w_ref, wb_ref)])
        w_ref = wb_ref
    b = lax.div(pl.program_id(0), seq // tm)
    gate = gt_ref[0, pl.ds(b, 1), :]
    for c in range(tm // MM_ROWS):
        rows = slice(c * MM_ROWS, (c + 1) * MM_ROWS)
        y = jnp.dot(a_ref[rows, :], w_ref[...], preferred_element_type=F32)
        xn = x_ref[rows, :] + gate * y
        if mode in ("x", "xh"):
            xo_ref[rows, :] = xn
        if mode == "xh":
            shift, scale = sh_ref[0, pl.ds(b, 1), :], sc_ref[0, pl.ds(b, 1), :]
            ho_ref[rows, :] = (_rms(xn) * (ng_ref[0] * (1.0 + scale)) + shift).astype(BF16)
        if mode == "final":
            fo_ref[rows, :] = _rms(xn) * ng_ref[0]


def _mm_res(a, w_b, wi, xf, mod, gate_at, seq, *, mode, norm=None, norm_idx=None, next_layer=None):
    m, kdim = a.shape
    d = xf.shape[1]
    tm = 2 * MM_ROWS if kdim <= d else MM_ROWS
    cast = w_b.dtype != BF16
    kern = functools.partial(_mm_res_kernel, mode=mode, tm=tm, seq=seq, cast=cast)
    in_specs = [pl.BlockSpec((tm, kdim), lambda i: (i, 0)),
                pl.BlockSpec((None, kdim, d), lambda i: (wi, 0, 0), pipeline_mode=pl.Buffered(1)),
                pl.BlockSpec((tm, d), lambda i: (i, 0)),
                _mod_spec(gate_at[0], gate_at[1], d)]
    args = [a, w_b, xf, mod]
    row_out = pl.BlockSpec((tm, d), lambda i: (i, 0))
    if mode == "x":
        out_specs, out_shape = [row_out], [jax.ShapeDtypeStruct((m, d), F32)]
    elif mode == "xh":
        in_specs += [_row_spec(norm_idx, d), _mod_spec(next_layer[0], next_layer[1], d),
                     _mod_spec(next_layer[0], next_layer[1] + 1, d)]
        args += [norm, mod, mod]
        out_specs = [row_out, row_out]
        out_shape = [jax.ShapeDtypeStruct((m, d), F32), jax.ShapeDtypeStruct((m, d), BF16)]
    else:
        in_specs += [_row_spec(norm_idx, d)]
        args += [norm]
        out_specs, out_shape = [row_out], [jax.ShapeDtypeStruct((m, d), F32)]
    return pl.pallas_call(
        kern, grid=(m // tm,), in_specs=in_specs, out_specs=out_specs, out_shape=out_shape,
        scratch_shapes=[pltpu.VMEM((kdim, d), BF16)] if cast else [],
        compiler_params=_cparams(1), name=f"mmres_{mode}_{kdim}",
    )(*args)


def _shift_rows(z, carry_row):
    rolled = pltpu.roll(z, 1, axis=0)
    first = jnp.where(lax.broadcasted_iota(jnp.int32, (SUBLANES, z.shape[1]), 0) == 0,
                      carry_row, rolled[0:SUBLANES, :])
    return jnp.concatenate([first, rolled[SUBLANES:, :]], axis=0)


def _causal_conv3(u, cw_ref, b_ref, carry):
    z0 = cw_ref[0, 0:1, :] * u
    z1 = cw_ref[0, 1:2, :] * u + _shift_rows(z0, carry[0])
    out = cw_ref[0, 2:3, :] * u + _shift_rows(z1, carry[1]) + b_ref[0]
    return out, (z0[-1:, :], z1[-1:, :])


def _up_kernel(a_ref, wv_ref, wg_ref, cwv_ref, cwg_ref, bv_ref, bg_ref, wd_ref, o_ref, wdb_ref,
               wvb_ref, wgb_ref, *, rc):
    wdb_ref[...] = wd_ref[...].astype(BF16)
    _cast_rows(pl.program_id(1) == 0, [(wv_ref, wvb_ref), (wg_ref, wgb_ref)])

    seq, tf = o_ref.shape
    zero = jnp.zeros((1, tf), F32)
    carry_v = carry_g = (zero, zero)

    def project(c):
        a = a_ref[c * rc:(c + 1) * rc, :]
        return (jnp.dot(a, wvb_ref[...], preferred_element_type=F32),
                jnp.dot(a, wgb_ref[...], preferred_element_type=F32))

    nxt = project(0)
    for c in range(seq // rc):
        uv, ug = nxt
        if c + 1 < seq // rc:
            nxt = project(c + 1)
        val, carry_v = _causal_conv3(uv, cwv_ref, bv_ref, carry_v)
        gate, carry_g = _causal_conv3(ug, cwg_ref, bg_ref, carry_g)
        o_ref[c * rc:(c + 1) * rc, :] = (_silu(gate) * val).astype(BF16)


def _up(h, w_up, conv_w, conv_b3, w_down, layer, seq):
    m, d = h.shape
    f = w_up.shape[2] // 2
    tf = 512
    nf = f // tf
    nbat = m // seq
    wd_rows = f // (nf * nbat)
    assert wd_rows * nf * nbat == f and wd_rows % 16 == 0
    kern = functools.partial(_up_kernel, rc=256)
    return pl.pallas_call(
        kern,
        grid=(nf, nbat),
        in_specs=[pl.BlockSpec((seq, d), lambda j, b: (b, 0)),
                  pl.BlockSpec((None, d, tf), lambda j, b: (layer, 0, j)),
                  pl.BlockSpec((None, d, tf), lambda j, b: (layer, 0, nf + j)),
                  pl.BlockSpec((1, CONV_WIDTH, tf), lambda j, b: (layer, 0, j)),
                  pl.BlockSpec((1, CONV_WIDTH, tf), lambda j, b: (layer, 0, nf + j)),
                  pl.BlockSpec((1, 1, tf), lambda j, b: (layer, 0, j)),
                  pl.BlockSpec((1, 1, tf), lambda j, b: (layer, 0, nf + j)),
                  pl.BlockSpec((None, wd_rows, d), lambda j, b: (layer, j * nbat + b, 0))],
        out_specs=[pl.BlockSpec((seq, tf), lambda j, b: (b, j)),
                   pl.BlockSpec((None, wd_rows, d), lambda j, b: (0, j * nbat + b, 0))],
        out_shape=[jax.ShapeDtypeStruct((m, f), BF16), jax.ShapeDtypeStruct((1, f, d), BF16)],
        scratch_shapes=[pltpu.VMEM((d, tf), BF16), pltpu.VMEM((d, tf), BF16)],
        compiler_params=_cparams(2),
        name=f"up{layer}",
    )(h, w_up, w_up, conv_w, conv_w, conv_b3, conv_b3, w_down)


def _qkv_kernel(a_ref, w_ref, o_ref, wb_ref, *, d_model):
    _cast_rows(pl.program_id(1) == 0, [(w_ref, wb_ref)])
    y = jnp.dot(a_ref[...], wb_ref[...], preferred_element_type=F32)
    tn = o_ref.shape[1]
    scale = jnp.where(pl.program_id(0) * tn < d_model, HEAD_DIM ** -0.5 * LOG2E, 1.0)
    o_ref[...] = (y * scale).astype(BF16)


def _qkv(h, w_qkv, li):
    m, d = h.shape
    n = w_qkv.shape[2]
    tm, tn = 1024, 1024
    assert d % tn == 0
    return pl.pallas_call(
        functools.partial(_qkv_kernel, d_model=d),
        grid=(n // tn, m // tm),
        in_specs=[pl.BlockSpec((tm, d), lambda j, i: (i, 0)),
                  pl.BlockSpec((None, d, tn), lambda j, i: (li, 0, j))],
        out_specs=pl.BlockSpec((tm, tn), lambda j, i: (i, j)),
        out_shape=jax.ShapeDtypeStruct((m, n), BF16),
        scratch_shapes=[pltpu.VMEM((d, tn), BF16)],
        compiler_params=_cparams(2),
        name="qkv",
    )(h, w_qkv)


def _split_bf16(v):
    pieces, rest = [], v
    for _ in range(N_SPLIT):
        piece = rest.astype(BF16).astype(F32)
        pieces.append(piece)
        rest = rest - piece
    return pieces


def _key_features(seq):
    nb = seq // MOBA_BLOCK
    assert nb + N_SPLIT * nb <= HEAD_DIM
    blk_of = np.arange(seq) // MOBA_BLOCK
    kf = np.zeros((seq, HEAD_DIM), np.float32)
    for n in range(nb):
        kf[blk_of == n, n] = NEG
    for j in range(nb):
        kf[blk_of <= j - 2, nb + N_SPLIT * j:nb + N_SPLIT * (j + 1)] = 1.0
    return jnp.asarray(kf, dtype=BF16)


def _attn_head(tab_ref, head, q_ref, k_ref, v_ref, bias_ref, kf_ref, o_ref, kx_ref, vx_ref):
    blk = MOBA_BLOCK
    seq, hd = q_ref.shape
    nb = seq // blk
    nt = (((1,), (1,)), ((), ()))

    kx_ref[:, 0:hd] = k_ref[...]
    kx_ref[:, hd:] = kf_ref[...]
    vx_ref[:, 0:hd] = v_ref[...]
    vx_ref[:, hd:] = jnp.ones((seq, hd), BF16)

    lane = lax.broadcasted_iota(jnp.int32, (1, hd), 1)
    far_p = _split_bf16(jnp.full((1, hd), tab_ref[REL_BUCKETS - 1, head] * LOG2E, F32))
    kmean = jnp.sum(k_ref[...].astype(F32).reshape(nb, blk, hd), axis=1) * (1.0 / blk)
    ks = jnp.concatenate(_split_bf16(kmean) + [jnp.zeros((nb, hd), F32)], axis=0).astype(BF16)
    n_idx = lax.broadcasted_iota(jnp.int32, (nb, blk), 0)

    def scores(j):
        span = (j + 1) * blk
        qj = q_ref[j * blk:(j + 1) * blk, :]
        if j < 2:
            return lax.dot_general(qj, k_ref[0:span, :], nt, preferred_element_type=F32)
        farvec = jnp.zeros((1, hd), F32)
        for piece in range(N_SPLIT):
            farvec = jnp.where(lane == nb + N_SPLIT * j + piece, far_p[piece], farvec)
        if j > MOBA_TOPK:
            gt = lax.dot_general(ks, qj, nt, preferred_element_type=F32)
            gate = gt[0:nb] + gt[nb:2 * nb] + gt[2 * nb:3 * nb]
            rank = jnp.zeros((nb, blk), jnp.int32)
            for mm in range(j):
                gm = gate[mm:mm + 1, :]
                beats = (gm > gate) | ((gm == gate) & (mm < n_idx))
                rank = rank + beats.astype(jnp.int32)
            drop_t = ((rank >= MOBA_TOPK) & (n_idx < j)).astype(F32)
            drop = jnp.concatenate([drop_t, jnp.zeros((hd - nb, blk), F32)], axis=0).T
            qf = (drop + farvec).astype(BF16)
        else:
            qf = jnp.broadcast_to(farvec, (blk, hd)).astype(BF16)
        qx = jnp.concatenate([qj, qf], axis=1)
        return lax.dot_general(qx, kx_ref[0:span, :], nt, preferred_element_type=F32)

    order = list(reversed(range(nb)))
    s_next = scores(order[0])
    for i, j in enumerate(order):
        span = (j + 1) * blk
        s = s_next
        if i + 1 < nb:
            s_next = scores(order[i + 1])
        logits = []
        for n in range(j + 1):
            lg = s[:, n * blk:(n + 1) * blk]
            if n == j:
                lg = lg + bias_ref[1]
            elif n == j - 1:
                lg = lg + bias_ref[0]
            logits.append(lg)
        mx = jnp.max(functools.reduce(jnp.maximum, logits), axis=-1, keepdims=True)
        p = jnp.concatenate([jnp.exp2(lg - mx).astype(BF16) for lg in logits], axis=-1)
        acc = jnp.dot(p, vx_ref[0:span, :], preferred_element_type=F32)
        o_ref[j * blk:(j + 1) * blk, :] = (acc[:, 0:hd] / acc[:, hd:]).astype(BF16)


def _attn_kernel(tab_ref, q_ref, k_ref, v_ref, bias_ref, kf_ref, o_ref, kx_ref, vx_ref):
    for hh in range(ATTN_HEADS_PER_STEP):
        cols = slice(hh * HEAD_DIM, (hh + 1) * HEAD_DIM)
        _attn_head(tab_ref, pl.program_id(1) * ATTN_HEADS_PER_STEP + hh,
                   q_ref.at[:, cols], k_ref.at[:, cols], v_ref.at[:, cols], bias_ref.at[hh], kf_ref,
                   o_ref.at[:, cols], kx_ref.at[hh], vx_ref.at[hh])


def _attn(qkv, bias, rel_table, seq):
    m, n3 = qkv.shape
    d = n3 // 3
    hps = ATTN_HEADS_PER_STEP
    groups = d // (hps * HEAD_DIM)
    blk = MOBA_BLOCK
    assert seq % blk == 0
    head_cols = pl.BlockSpec((seq, hps * HEAD_DIM), lambda b, g: (b, g))
    return pl.pallas_call(
        _attn_kernel,
        grid=(m // seq, groups),
        in_specs=[pl.BlockSpec(memory_space=pltpu.SMEM),
                  head_cols,
                  pl.BlockSpec((seq, hps * HEAD_DIM), lambda b, g: (b, groups + g)),
                  pl.BlockSpec((seq, hps * HEAD_DIM), lambda b, g: (b, 2 * groups + g)),
                  pl.BlockSpec((hps, 2, blk, blk), lambda b, g: (g, 0, 0, 0)),
                  pl.BlockSpec((seq, HEAD_DIM), lambda b, g: (0, 0))],
        out_specs=head_cols,
        out_shape=jax.ShapeDtypeStruct((m, d), BF16),
        scratch_shapes=[pltpu.VMEM((hps, seq, 2 * HEAD_DIM), BF16), pltpu.VMEM((hps, seq, 2 * HEAD_DIM), BF16)],
        compiler_params=_cparams(2),
        name="attn",
    )(rel_table, qkv, qkv, qkv, bias, _key_features(seq))


def kernel(x, c, norm_g, w_ada, b_ada, pool_w, pool_scale, w_qkv, w_o, rel_table,
           w_up, conv_w, conv_b, w_down, final_g):
    bsz, seq, d = x.shape
    depth = w_ada.shape[0]
    m = bsz * seq

    mod = _ada(c, w_ada, b_ada)
    bias = _relbias(rel_table)

    pool_w_b = pool_w.astype(BF16)
    norm_g3 = norm_g.reshape(depth * 2, 1, d)
    pool_scale3 = pool_scale.reshape(-1, 1, d)
    conv_b3 = conv_b.reshape(depth, 1, -1)
    final_g3 = final_g.reshape(1, 1, d)

    xf = x.reshape(m, d)
    h = None
    out = None
    for i in range(depth):
        li = i // N_MIXERS
        if i % N_MIXERS == 0:
            xf, h2 = _pool(xf, mod, norm_g3, pool_w_b, pool_scale3, i, li, seq)
        else:
            o = _attn(_qkv(h, w_qkv, li), bias, rel_table, seq)
            xf, h2 = _mm_res(o, w_o, li, xf, mod, (i, 2), seq, mode="xh",
                             norm=norm_g3, norm_idx=i * 2 + 1, next_layer=(i, 3))
        a, w_down_b = _up(h2, w_up, conv_w, conv_b3, w_down, i, seq)
        if i == depth - 1:
            (out,) = _mm_res(a, w_down_b, 0, xf, mod, (i, 5), seq, mode="final",
                             norm=final_g3, norm_idx=0)
        elif (i + 1) % N_MIXERS == 0:
            (xf,) = _mm_res(a, w_down_b, 0, xf, mod, (i, 5), seq, mode="x")
        else:
            xf, h = _mm_res(a, w_down_b, 0, xf, mod, (i, 5), seq, mode="xh",
                            norm=norm_g3, norm_idx=(i + 1) * 2, next_layer=(i + 1, 0))
    return out.reshape(bsz, seq, d)
```

```python
import functools
import math

import numpy as np
import jax
import jax.numpy as jnp
from jax import lax
from jax.experimental import pallas as pl
from jax.experimental.pallas import tpu as pltpu

F32 = jnp.float32
BF16 = jnp.bfloat16

N_MIXERS = 2
POOL_WINDOWS = (2, 4, 8, 16)
HEAD_DIM = 128
MOBA_BLOCK = 256
MOBA_TOPK = 3
REL_BUCKETS = 32
REL_MAX_DIST = 128
CONV_WIDTH = 3
EPS = 1e-6
NEG = -1e30
LOG2E = math.log2(math.e)
N_SPLIT = 3
ATTN_HEADS_PER_STEP = 4
N_MOD = 6
SUBLANES = 8
MOD_ROWS = SUBLANES
POOL_HALO = 16
CHUNK = 256
CHUNK_SLABS = CHUNK // SUBLANES
CAST_ROWS = 256
MM_ROWS = 256
VMEM_LIMIT = 56 * 1024 * 1024


def _cparams(n_axes):
    return pltpu.CompilerParams(dimension_semantics=("arbitrary",) * n_axes,
                                vmem_limit_bytes=VMEM_LIMIT)


def _silu(v):
    return v / (1.0 + jnp.exp(-v))


def _rms(xv):
    return xv * lax.rsqrt(jnp.mean(xv * xv, axis=-1, keepdims=True) + EPS)


def _cast_rows(pred, pairs):
    n_rows = pairs[0][0].shape[0]
    assert all(src.shape[0] == n_rows for src, _ in pairs) and n_rows % CAST_ROWS == 0

    def body(r, carry):
        rows = pl.ds(pl.multiple_of(r * CAST_ROWS, CAST_ROWS), CAST_ROWS)
        for src, dst in pairs:
            dst[rows, :] = src[rows, :].astype(BF16)
        return carry

    lax.fori_loop(0, jnp.where(pred, n_rows // CAST_ROWS, 0), body, 0)


def _ada_kernel(c_ref, w_ref, b_ref, o_ref):
    cond = _silu(c_ref[...]).astype(BF16)
    o_ref[0] = jnp.dot(cond, w_ref[0].astype(BF16), preferred_element_type=F32) + b_ref[0]


def _ada(c, w_ada, b_ada):
    depth, d, _ = w_ada.shape
    cp = jnp.pad(c, ((0, MOD_ROWS - c.shape[0]), (0, 0)))
    b3 = b_ada.reshape(depth * N_MOD, 1, d)
    return pl.pallas_call(
        _ada_kernel,
        grid=(depth, N_MOD),
        in_specs=[pl.BlockSpec((MOD_ROWS, d), lambda i, k: (0, 0)),
                  pl.BlockSpec((1, d, d), lambda i, k: (i, 0, k)),
                  pl.BlockSpec((1, 1, d), lambda i, k: (i * N_MOD + k, 0, 0))],
        out_specs=pl.BlockSpec((1, MOD_ROWS, d), lambda i, k: (i * N_MOD + k, 0, 0)),
        out_shape=jax.ShapeDtypeStruct((depth * N_MOD, MOD_ROWS, d), F32),
        compiler_params=_cparams(2),
        name="ada",
    )(cp, w_ada, b3)


def _mod_spec(layer, k, d):
    idx = layer * N_MOD + k
    return pl.BlockSpec((1, MOD_ROWS, d), lambda *_: (idx, 0, 0))


def _row_spec(idx, d):
    return pl.BlockSpec((1, 1, d), lambda *_: (idx, 0, 0))


def _rel_bucket_np(dist):
    n = np.maximum(dist, 0)
    max_exact = REL_BUCKETS // 2
    nf = np.maximum(n, 1).astype(np.float32)
    large = max_exact + (np.log(nf / np.float32(max_exact)) / np.float32(np.log(REL_MAX_DIST / max_exact))
                         * np.float32(REL_BUCKETS - max_exact)).astype(np.int32)
    large = np.minimum(large, REL_BUCKETS - 1)
    return np.where(n < max_exact, n, large).astype(np.int32)


def _chunk_step_of_row():
    row = np.arange(CHUNK)
    return (row % SUBLANES) * CHUNK_SLABS + row // SUBLANES


def _to_chunk_order(x3, inverse=False):
    bsz, seq, d = x3.shape
    a, b = (CHUNK_SLABS, SUBLANES) if inverse else (SUBLANES, CHUNK_SLABS)
    return x3.reshape(bsz, seq // CHUNK, a, b, d).transpose(0, 1, 3, 2, 4).reshape(bsz, seq, d)


def _bucket_tiles():
    assert MOBA_BLOCK == CHUNK
    step = _chunk_step_of_row()
    t = step[:, None]
    s = step[None, :]
    prev = _rel_bucket_np(MOBA_BLOCK + t - s)
    own = np.where(t - s >= 0, _rel_bucket_np(t - s), -1)
    assert _rel_bucket_np(np.arange(MOBA_BLOCK + 1, 16 * MOBA_BLOCK)).min() == REL_BUCKETS - 1
    return np.stack([prev, own]).astype(np.int32)


def _relbias_kernel(tab_ref, bkt_ref, o_ref):
    h = pl.program_id(0)
    for t in range(2):
        bk = bkt_ref[t]
        acc = jnp.full(bk.shape, NEG, F32)
        for b in range(REL_BUCKETS):
            acc = jnp.where(bk == b, tab_ref[b, h] * LOG2E, acc)
        o_ref[0, t] = acc


def _relbias(rel_table):
    n_heads = rel_table.shape[1]
    bkt = jnp.asarray(_bucket_tiles())
    blk = MOBA_BLOCK
    return pl.pallas_call(
        _relbias_kernel,
        grid=(n_heads,),
        in_specs=[pl.BlockSpec(memory_space=pltpu.SMEM),
                  pl.BlockSpec((2, blk, blk), lambda h: (0, 0, 0))],
        out_specs=pl.BlockSpec((1, 2, blk, blk), lambda h: (h, 0, 0, 0)),
        out_shape=jax.ShapeDtypeStruct((n_heads, 2, blk, blk), F32),
        compiler_params=_cparams(1),
        name="relbias",
    )(rel_table, bkt)


def _shift_time(cur, prev_rows, j):
    keep = (CHUNK_SLABS - j) * SUBLANES
    sub = lax.broadcasted_iota(jnp.int32, (SUBLANES, cur.shape[1]), 0)
    fixed = []
    for k in range(j):
        slab = cur[keep + k * SUBLANES:keep + (k + 1) * SUBLANES, :]
        fixed.append(jnp.where(sub == 0, prev_rows[k], pltpu.roll(slab, 1, axis=0)))
    return jnp.concatenate(fixed + [cur[:keep, :]], axis=0)


def _pool_kernel(x_ref, xh_ref, ng1_ref, sh1_ref, sc1_ref, gt1_ref, w_ref, ps_ref,
                 ng2_ref, sh2_ref, sc2_ref, xo_ref, ho_ref, hs_ref, *, tm, seq):
    i = pl.program_id(0)
    tiles_per_seq = seq // tm
    b = lax.div(i, tiles_per_seq)
    row0 = lax.rem(i, tiles_per_seq) * tm
    group = xo_ref.shape[1] // len(POOL_WINDOWS)

    def row(ref):
        return ref[0, pl.ds(b, 1), :]

    def normmod(xv, g, shift, scale):
        return _rms(xv) * (g * (1.0 + scale)) + shift

    def last_steps(ref, tail_row0):
        sub = lax.broadcasted_iota(jnp.int32, (SUBLANES, ref.shape[1]), 0)
        out = []
        for m0 in range(0, POOL_HALO, SUBLANES):
            acc = jnp.zeros((SUBLANES, ref.shape[1]), F32)
            for r in range(SUBLANES):
                r0 = tail_row0 + (m0 + r) * SUBLANES
                slab = ref[r0:r0 + SUBLANES, :]
                if r + 1 < SUBLANES:
                    slab = pltpu.roll(slab, r + 1, axis=0)
                acc = jnp.where(sub == r, slab, acc)
            out.append(acc)
        return jnp.concatenate(out, axis=0)

    g1, shift1, scale1 = ng1_ref[0], row(sh1_ref), row(sc1_ref)
    hs_ref[...] = normmod(x_ref[...], g1, shift1, scale1)
    halo = normmod(last_steps(xh_ref, 0), g1, shift1, scale1)
    halo = jnp.where(row0 == 0, 0.0, halo)

    stored = lax.broadcasted_iota(jnp.int32, (CHUNK, 1), 0)
    step = lax.rem(stored, SUBLANES) * CHUNK_SLABS + lax.div(stored, SUBLANES)
    gate1 = row(gt1_ref) * ps_ref[0]
    for c in range(tm // CHUNK):
        rows = slice(c * CHUNK, (c + 1) * CHUNK)
        before = halo if c == 0 else last_steps(hs_ref, c * CHUNK - POOL_HALO * SUBLANES)
        pos = row0 + c * CHUNK + step
        for g, win in enumerate(POOL_WINDOWS):
            cols = slice(g * group, (g + 1) * group)
            h0 = hs_ref[rows, cols]
            acc, prev, span = h0, before[:, cols], 1
            while span < win:
                prev_rows = [prev[POOL_HALO - span + k:POOL_HALO - span + k + 1, :] for k in range(span)]
                acc = acc + _shift_time(acc, prev_rows, span)
                prev = prev + pltpu.roll(prev, span, axis=0)
                span *= 2
            cnt = jnp.minimum(pos + 1, win).astype(F32)
            dlt = (acc / cnt - h0).astype(BF16)
            y = jnp.dot(dlt, w_ref[g], preferred_element_type=F32)
            xo_ref[rows, cols] = x_ref[rows, cols] + gate1[:, cols] * y

    ho_ref[...] = normmod(xo_ref[...], ng2_ref[0], row(sh2_ref), row(sc2_ref)).astype(BF16)


def _pool(xf, mod, norm_g3, pool_w_b, pool_scale3, layer, li, seq):
    m, d = xf.shape
    tm = 2 * CHUNK
    n_grp, grp, _ = pool_w_b.shape[1:]
    halo_rows = POOL_HALO * SUBLANES
    halo_blocks = tm // halo_rows
    assert all(w & (w - 1) == 0 and w <= POOL_HALO for w in POOL_WINDOWS)
    kern = functools.partial(_pool_kernel, tm=tm, seq=seq)
    return pl.pallas_call(
        kern,
        grid=(m // tm,),
        in_specs=[pl.BlockSpec((tm, d), lambda i: (i, 0)),
                  pl.BlockSpec((halo_rows, d), lambda i: (jnp.maximum(i * halo_blocks - 1, 0), 0)),
                  _row_spec(layer * 2, d),
                  _mod_spec(layer, 0, d), _mod_spec(layer, 1, d), _mod_spec(layer, 2, d),
                  pl.BlockSpec((None, n_grp, grp, grp), lambda i: (li, 0, 0, 0)),
                  _row_spec(li, d),
                  _row_spec(layer * 2 + 1, d),
                  _mod_spec(layer, 3, d), _mod_spec(layer, 4, d)],
        out_specs=[pl.BlockSpec((tm, d), lambda i: (i, 0)),
                   pl.BlockSpec((tm, d), lambda i: (i, 0))],
        out_shape=[jax.ShapeDtypeStruct((m, d), F32), jax.ShapeDtypeStruct((m, d), BF16)],
        scratch_shapes=[pltpu.VMEM((tm, d), F32)],
        compiler_params=_cparams(1),
        name=f"pool{layer}",
    )(xf, xf, norm_g3, mod, mod, mod, pool_w_b, pool_scale3, norm_g3, mod, mod)


def _mm_res_kernel(*refs, mode, tm, seq, cast):
    if cast:
        *refs, wb_ref = refs
    if mode == "x":
        a_ref, w_ref, x_ref, gt_ref, xo_ref = refs
    elif mode == "xh":
        a_ref, w_ref, x_ref, gt_ref, ng_ref, sh_ref, sc_ref, xo_ref, ho_ref = refs
    else:
        a_ref, w_ref, x_ref, gt_ref, ng_ref, fo_ref = refs
    if cast:
        _cast_rows(pl.program_id(0) == 0, [(w_ref, wb_ref)])
        w_ref = wb_ref
    b = lax.div(pl.program_id(0), seq // tm)
    gate = gt_ref[0, pl.ds(b, 1), :]
    for c in range(tm // MM_ROWS):
        rows = slice(c * MM_ROWS, (c + 1) * MM_ROWS)
        y = jnp.dot(a_ref[rows, :], w_ref[...], preferred_element_type=F32)
        xn = x_ref[rows, :] + gate * y
        if mode in ("x", "xh"):
            xo_ref[rows, :] = xn
        if mode == "xh":
            shift, scale = sh_ref[0, pl.ds(b, 1), :], sc_ref[0, pl.ds(b, 1), :]
            ho_ref[rows, :] = (_rms(xn) * (ng_ref[0] * (1.0 + scale)) + shift).astype(BF16)
        if mode == "final":
            fo_ref[rows, :] = _rms(xn) * ng_ref[0]


def _mm_res(a, w_b, wi, xf, mod, gate_at, seq, *, mode, norm=None, norm_idx=None, next_layer=None):
    m, kdim = a.shape
    d = xf.shape[1]
    tm = 2 * MM_ROWS if kdim <= d else MM_ROWS
    cast = w_b.dtype != BF16
    kern = functools.partial(_mm_res_kernel, mode=mode, tm=tm, seq=seq, cast=cast)
    in_specs = [pl.BlockSpec((tm, kdim), lambda i: (i, 0)),
                pl.BlockSpec((None, kdim, d), lambda i: (wi, 0, 0), pipeline_mode=pl.Buffered(1)),
                pl.BlockSpec((tm, d), lambda i: (i, 0)),
                _mod_spec(gate_at[0], gate_at[1], d)]
    args = [a, w_b, xf, mod]
    row_out = pl.BlockSpec((tm, d), lambda i: (i, 0))
    if mode == "x":
        out_specs, out_shape = [row_out], [jax.ShapeDtypeStruct((m, d), F32)]
    elif mode == "xh":
        in_specs += [_row_spec(norm_idx, d), _mod_spec(next_layer[0], next_layer[1], d),
                     _mod_spec(next_layer[0], next_layer[1] + 1, d)]
        args += [norm, mod, mod]
        out_specs = [row_out, row_out]
        out_shape = [jax.ShapeDtypeStruct((m, d), F32), jax.ShapeDtypeStruct((m, d), BF16)]
    else:
        in_specs += [_row_spec(norm_idx, d)]
        args += [norm]
        out_specs, out_shape = [row_out], [jax.ShapeDtypeStruct((m, d), F32)]
    return pl.pallas_call(
        kern, grid=(m // tm,), in_specs=in_specs, out_specs=out_specs, out_shape=out_shape,
        scratch_shapes=[pltpu.VMEM((kdim, d), BF16)] if cast else [],
        compiler_params=_cparams(1), name=f"mmres_{mode}_{kdim}",
    )(*args)


def _causal_conv3(u, cw_ref, b_ref, carry):
    z0 = cw_ref[0, 0:1, :] * u
    z1 = cw_ref[0, 1:2, :] * u + _shift_time(z0, [carry[0]], 1)
    out = cw_ref[0, 2:3, :] * u + _shift_time(z1, [carry[1]], 1) + b_ref[0]
    return out, (z0[-1:, :], z1[-1:, :])


def _up_kernel(a_ref, wv_ref, wg_ref, cwv_ref, cwg_ref, bv_ref, bg_ref, wd_ref, o_ref, wdb_ref,
               wvb_ref, wgb_ref, *, rc):
    wdb_ref[...] = wd_ref[...].astype(BF16)
    _cast_rows(pl.program_id(1) == 0, [(wv_ref, wvb_ref), (wg_ref, wgb_ref)])

    seq, tf = o_ref.shape
    zero = jnp.zeros((1, tf), F32)
    carry_v = carry_g = (zero, zero)

    def project(c):
        a = a_ref[c * rc:(c + 1) * rc, :]
        return (jnp.dot(a, wvb_ref[...], preferred_element_type=F32),
                jnp.dot(a, wgb_ref[...], preferred_element_type=F32))

    nxt = project(0)
    for c in range(seq // rc):
        uv, ug = nxt
        if c + 1 < seq // rc:
            nxt = project(c + 1)
        val, carry_v = _causal_conv3(uv, cwv_ref, bv_ref, carry_v)
        gate, carry_g = _causal_conv3(ug, cwg_ref, bg_ref, carry_g)
        o_ref[c * rc:(c + 1) * rc, :] = (_silu(gate) * val).astype(BF16)


def _up(h, w_up, conv_w, conv_b3, w_down, layer, seq):
    m, d = h.shape
    f = w_up.shape[2] // 2
    tf = 512
    nf = f // tf
    nbat = m // seq
    wd_rows = f // (nf * nbat)
    assert wd_rows * nf * nbat == f and wd_rows % 16 == 0
    kern = functools.partial(_up_kernel, rc=CHUNK)
    return pl.pallas_call(
        kern,
        grid=(nf, nbat),
        in_specs=[pl.BlockSpec((seq, d), lambda j, b: (b, 0)),
                  pl.BlockSpec((None, d, tf), lambda j, b: (layer, 0, j)),
                  pl.BlockSpec((None, d, tf), lambda j, b: (layer, 0, nf + j)),
                  pl.BlockSpec((1, CONV_WIDTH, tf), lambda j, b: (layer, 0, j)),
                  pl.BlockSpec((1, CONV_WIDTH, tf), lambda j, b: (layer, 0, nf + j)),
                  pl.BlockSpec((1, 1, tf), lambda j, b: (layer, 0, j)),
                  pl.BlockSpec((1, 1, tf), lambda j, b: (layer, 0, nf + j)),
                  pl.BlockSpec((None, wd_rows, d), lambda j, b: (layer, j * nbat + b, 0))],
        out_specs=[pl.BlockSpec((seq, tf), lambda j, b: (b, j)),
                   pl.BlockSpec((None, wd_rows, d), lambda j, b: (0, j * nbat + b, 0))],
        out_shape=[jax.ShapeDtypeStruct((m, f), BF16), jax.ShapeDtypeStruct((1, f, d), BF16)],
        scratch_shapes=[pltpu.VMEM((d, tf), BF16), pltpu.VMEM((d, tf), BF16)],
        compiler_params=_cparams(2),
        name=f"up{layer}",
    )(h, w_up, w_up, conv_w, conv_w, conv_b3, conv_b3, w_down)


def _qkv_kernel(a_ref, w_ref, o_ref, wb_ref, *, d_model):
    _cast_rows(pl.program_id(1) == 0, [(w_ref, wb_ref)])
    y = jnp.dot(a_ref[...], wb_ref[...], preferred_element_type=F32)
    tn = o_ref.shape[1]
    scale = jnp.where(pl.program_id(0) * tn < d_model, HEAD_DIM ** -0.5 * LOG2E, 1.0)
    o_ref[...] = (y * scale).astype(BF16)


def _qkv(h, w_qkv, li):
    m, d = h.shape
    n = w_qkv.shape[2]
    tm, tn = 1024, 1024
    assert d % tn == 0
    return pl.pallas_call(
        functools.partial(_qkv_kernel, d_model=d),
        grid=(n // tn, m // tm),
        in_specs=[pl.BlockSpec((tm, d), lambda j, i: (i, 0)),
                  pl.BlockSpec((None, d, tn), lambda j, i: (li, 0, j))],
        out_specs=pl.BlockSpec((tm, tn), lambda j, i: (i, j)),
        out_shape=jax.ShapeDtypeStruct((m, n), BF16),
        scratch_shapes=[pltpu.VMEM((d, tn), BF16)],
        compiler_params=_cparams(2),
        name="qkv",
    )(h, w_qkv)


def _split_bf16(v):
    pieces, rest = [], v
    for _ in range(N_SPLIT):
        piece = rest.astype(BF16).astype(F32)
        pieces.append(piece)
        rest = rest - piece
    return pieces


def _key_features(seq):
    nb = seq // MOBA_BLOCK
    assert nb + N_SPLIT * nb <= HEAD_DIM
    blk_of = np.arange(seq) // MOBA_BLOCK
    kf = np.zeros((seq, HEAD_DIM), np.float32)
    for n in range(nb):
        kf[blk_of == n, n] = NEG
    for j in range(nb):
        kf[blk_of <= j - 2, nb + N_SPLIT * j:nb + N_SPLIT * (j + 1)] = 1.0
    return jnp.asarray(kf, dtype=BF16)


def _attn_head(tab_ref, head, q_ref, k_ref, v_ref, bias_ref, kf_ref, o_ref, kx_ref, vx_ref):
    blk = MOBA_BLOCK
    seq, hd = q_ref.shape
    nb = seq // blk
    nt = (((1,), (1,)), ((), ()))

    kx_ref[:, 0:hd] = k_ref[...]
    kx_ref[:, hd:] = kf_ref[...]
    vx_ref[:, 0:hd] = v_ref[...]
    vx_ref[:, hd:] = jnp.ones((seq, hd), BF16)

    lane = lax.broadcasted_iota(jnp.int32, (1, hd), 1)
    far_p = _split_bf16(jnp.full((1, hd), tab_ref[REL_BUCKETS - 1, head] * LOG2E, F32))
    kmean = jnp.sum(k_ref[...].astype(F32).reshape(nb, blk, hd), axis=1) * (1.0 / blk)
    ks = jnp.concatenate(_split_bf16(kmean) + [jnp.zeros((nb, hd), F32)], axis=0).astype(BF16)
    n_idx = lax.broadcasted_iota(jnp.int32, (nb, blk), 0)

    def scores(j):
        span = (j + 1) * blk
        qj = q_ref[j * blk:(j + 1) * blk, :]
        if j < 2:
            return lax.dot_general(qj, k_ref[0:span, :], nt, preferred_element_type=F32)
        farvec = jnp.zeros((1, hd), F32)
        for piece in range(N_SPLIT):
            farvec = jnp.where(lane == nb + N_SPLIT * j + piece, far_p[piece], farvec)
        if j > MOBA_TOPK:
            gt = lax.dot_general(ks, qj, nt, preferred_element_type=F32)
            gate = gt[0:nb] + gt[nb:2 * nb] + gt[2 * nb:3 * nb]
            rank = jnp.zeros((nb, blk), jnp.int32)
            for mm in range(j):
                gm = gate[mm:mm + 1, :]
                beats = (gm > gate) | ((gm == gate) & (mm < n_idx))
                rank = rank + beats.astype(jnp.int32)
            drop_t = ((rank >= MOBA_TOPK) & (n_idx < j)).astype(F32)
            drop = jnp.concatenate([drop_t, jnp.zeros((hd - nb, blk), F32)], axis=0).T
            qf = (drop + farvec).astype(BF16)
        else:
            qf = jnp.broadcast_to(farvec, (blk, hd)).astype(BF16)
        qx = jnp.concatenate([qj, qf], axis=1)
        return lax.dot_general(qx, kx_ref[0:span, :], nt, preferred_element_type=F32)

    order = list(reversed(range(nb)))
    s_next = scores(order[0])
    for i, j in enumerate(order):
        span = (j + 1) * blk
        s = s_next
        if i + 1 < nb:
            s_next = scores(order[i + 1])
        logits = []
        for n in range(j + 1):
            lg = s[:, n * blk:(n + 1) * blk]
            if n == j:
                lg = lg + bias_ref[1]
            elif n == j - 1:
                lg = lg + bias_ref[0]
            logits.append(lg)
        mx = jnp.max(functools.reduce(jnp.maximum, logits), axis=-1, keepdims=True)
        p = jnp.concatenate([jnp.exp2(lg - mx).astype(BF16) for lg in logits], axis=-1)
        acc = jnp.dot(p, vx_ref[0:span, :], preferred_element_type=F32)
        o_ref[j * blk:(j + 1) * blk, :] = (acc[:, 0:hd] / acc[:, hd:]).astype(BF16)


def _attn_kernel(tab_ref, q_ref, k_ref, v_ref, bias_ref, kf_ref, o_ref, kx_ref, vx_ref):
    for hh in range(ATTN_HEADS_PER_STEP):
        cols = slice(hh * HEAD_DIM, (hh + 1) * HEAD_DIM)
        _attn_head(tab_ref, pl.program_id(1) * ATTN_HEADS_PER_STEP + hh,
                   q_ref.at[:, cols], k_ref.at[:, cols], v_ref.at[:, cols], bias_ref.at[hh], kf_ref,
                   o_ref.at[:, cols], kx_ref.at[hh], vx_ref.at[hh])


def _attn(qkv, bias, rel_table, seq):
    m, n3 = qkv.shape
    d = n3 // 3
    hps = ATTN_HEADS_PER_STEP
    groups = d // (hps * HEAD_DIM)
    blk = MOBA_BLOCK
    assert seq % blk == 0
    head_cols = pl.BlockSpec((seq, hps * HEAD_DIM), lambda b, g: (b, g))
    return pl.pallas_call(
        _attn_kernel,
        grid=(m // seq, groups),
        in_specs=[pl.BlockSpec(memory_space=pltpu.SMEM),
                  head_cols,
                  pl.BlockSpec((seq, hps * HEAD_DIM), lambda b, g: (b, groups + g)),
                  pl.BlockSpec((seq, hps * HEAD_DIM), lambda b, g: (b, 2 * groups + g)),
                  pl.BlockSpec((hps, 2, blk, blk), lambda b, g: (g, 0, 0, 0)),
                  pl.BlockSpec((seq, HEAD_DIM), lambda b, g: (0, 0))],
        out_specs=head_cols,
        out_shape=jax.ShapeDtypeStruct((m, d), BF16),
        scratch_shapes=[pltpu.VMEM((hps, seq, 2 * HEAD_DIM), BF16), pltpu.VMEM((hps, seq, 2 * HEAD_DIM), BF16)],
        compiler_params=_cparams(2),
        name="attn",
    )(rel_table, qkv, qkv, qkv, bias, _key_features(seq))


def kernel(x, c, norm_g, w_ada, b_ada, pool_w, pool_scale, w_qkv, w_o, rel_table,
           w_up, conv_w, conv_b, w_down, final_g):
    bsz, seq, d = x.shape
    depth = w_ada.shape[0]
    m = bsz * seq

    mod = _ada(c, w_ada, b_ada)
    bias = _relbias(rel_table)

    pool_w_b = pool_w.astype(BF16)
    norm_g3 = norm_g.reshape(depth * 2, 1, d)
    pool_scale3 = pool_scale.reshape(-1, 1, d)
    conv_b3 = conv_b.reshape(depth, 1, -1)
    final_g3 = final_g.reshape(1, 1, d)

    assert seq % CHUNK == 0
    xf = _to_chunk_order(x).reshape(m, d)
    h = None
    out = None
    for i in range(depth):
        li = i // N_MIXERS
        if i % N_MIXERS == 0:
            xf, h2 = _pool(xf, mod, norm_g3, pool_w_b, pool_scale3, i, li, seq)
        else:
            o = _attn(_qkv(h, w_qkv, li), bias, rel_table, seq)
            xf, h2 = _mm_res(o, w_o, li, xf, mod, (i, 2), seq, mode="xh",
                             norm=norm_g3, norm_idx=i * 2 + 1, next_layer=(i, 3))
        a, w_down_b = _up(h2, w_up, conv_w, conv_b3, w_down, i, seq)
        if i == depth - 1:
            (out,) = _mm_res(a, w_down_b, 0, xf, mod, (i, 5), seq, mode="final",
                             norm=final_g3, norm_idx=0)
        elif (i + 1) % N_MIXERS == 0:
            (xf,) = _mm_res(a, w_down_b, 0, xf, mod, (i, 5), seq, mode="x")
        else:
            xf, h = _mm_res(a, w_down_b, 0, xf, mod, (i, 5), seq, mode="xh",
                            norm=norm_g3, norm_idx=(i + 1) * 2, next_layer=(i + 1, 0))
    return _to_chunk_order(out.reshape(bsz, seq, d), inverse=True)
```

```python
import functools
import math

import numpy as np
import jax
import jax.numpy as jnp
from jax import lax
from jax.experimental import pallas as pl
from jax.experimental.pallas import tpu as pltpu

F32 = jnp.float32
BF16 = jnp.bfloat16

N_MIXERS = 2
POOL_WINDOWS = (2, 4, 8, 16)
HEAD_DIM = 128
MOBA_BLOCK = 256
MOBA_TOPK = 3
REL_BUCKETS = 32
REL_MAX_DIST = 128
CONV_WIDTH = 3
EPS = 1e-6
NEG = -1e30
LOG2E = math.log2(math.e)
N_SPLIT = 3
ATTN_HEADS_PER_STEP = 2
N_MOD = 6
SUBLANES = 8
MOD_ROWS = SUBLANES
POOL_HALO = 16
CHUNK = 256
CHUNK_SLABS = CHUNK // SUBLANES
CAST_ROWS = 256
MM_ROWS = 256
VMEM_LIMIT = 56 * 1024 * 1024


def _cparams(n_axes):
    return pltpu.CompilerParams(dimension_semantics=("arbitrary",) * n_axes,
                                vmem_limit_bytes=VMEM_LIMIT)


def _silu(v):
    return v / (1.0 + jnp.exp(-v))


def _rms(xv):
    return xv * lax.rsqrt(jnp.mean(xv * xv, axis=-1, keepdims=True) + EPS)


def _cast_rows(pred, pairs):
    n_rows = pairs[0][0].shape[0]
    assert all(src.shape[0] == n_rows for src, _ in pairs) and n_rows % CAST_ROWS == 0

    def body(r, carry):
        rows = pl.ds(pl.multiple_of(r * CAST_ROWS, CAST_ROWS), CAST_ROWS)
        for src, dst in pairs:
            dst[rows, :] = src[rows, :].astype(BF16)
        return carry

    lax.fori_loop(0, jnp.where(pred, n_rows // CAST_ROWS, 0), body, 0)


def _ada_kernel(c_ref, w_ref, b_ref, o_ref):
    cond = _silu(c_ref[...]).astype(BF16)
    o_ref[0] = jnp.dot(cond, w_ref[0].astype(BF16), preferred_element_type=F32) + b_ref[0]


def _ada(c, w_ada, b_ada):
    depth, d, _ = w_ada.shape
    cp = jnp.pad(c, ((0, MOD_ROWS - c.shape[0]), (0, 0)))
    b3 = b_ada.reshape(depth * N_MOD, 1, d)
    return pl.pallas_call(
        _ada_kernel,
        grid=(depth, N_MOD),
        in_specs=[pl.BlockSpec((MOD_ROWS, d), lambda i, k: (0, 0)),
                  pl.BlockSpec((1, d, d), lambda i, k: (i, 0, k)),
                  pl.BlockSpec((1, 1, d), lambda i, k: (i * N_MOD + k, 0, 0))],
        out_specs=pl.BlockSpec((1, MOD_ROWS, d), lambda i, k: (i * N_MOD + k, 0, 0)),
        out_shape=jax.ShapeDtypeStruct((depth * N_MOD, MOD_ROWS, d), F32),
        compiler_params=_cparams(2),
        name="ada",
    )(cp, w_ada, b3)


def _mod_spec(layer, k, d):
    idx = layer * N_MOD + k
    return pl.BlockSpec((1, MOD_ROWS, d), lambda *_: (idx, 0, 0))


def _row_spec(idx, d):
    return pl.BlockSpec((1, 1, d), lambda *_: (idx, 0, 0))


def _rel_bucket_np(dist):
    n = np.maximum(dist, 0)
    max_exact = REL_BUCKETS // 2
    nf = np.maximum(n, 1).astype(np.float32)
    large = max_exact + (np.log(nf / np.float32(max_exact)) / np.float32(np.log(REL_MAX_DIST / max_exact))
                         * np.float32(REL_BUCKETS - max_exact)).astype(np.int32)
    large = np.minimum(large, REL_BUCKETS - 1)
    return np.where(n < max_exact, n, large).astype(np.int32)


def _chunk_step_of_row():
    row = np.arange(CHUNK)
    return (row % SUBLANES) * CHUNK_SLABS + row // SUBLANES


def _to_chunk_order(x3, inverse=False):
    bsz, seq, d = x3.shape
    a, b = (CHUNK_SLABS, SUBLANES) if inverse else (SUBLANES, CHUNK_SLABS)
    return x3.reshape(bsz, seq // CHUNK, a, b, d).transpose(0, 1, 3, 2, 4).reshape(bsz, seq, d)


def _bucket_tiles():
    assert MOBA_BLOCK == CHUNK
    step = _chunk_step_of_row()
    t = step[:, None]
    s = step[None, :]
    prev = _rel_bucket_np(MOBA_BLOCK + t - s)
    own = np.where(t - s >= 0, _rel_bucket_np(t - s), -1)
    assert _rel_bucket_np(np.arange(MOBA_BLOCK + 1, 16 * MOBA_BLOCK)).min() == REL_BUCKETS - 1
    return np.stack([prev, own]).astype(np.int32)


def _relbias_kernel(tab_ref, bkt_ref, o_ref):
    h = pl.program_id(0)
    for t in range(2):
        bk = bkt_ref[t]
        acc = jnp.full(bk.shape, NEG, F32)
        for b in range(REL_BUCKETS):
            acc = jnp.where(bk == b, tab_ref[b, h] * LOG2E, acc)
        o_ref[0, t] = acc


def _relbias(rel_table):
    n_heads = rel_table.shape[1]
    bkt = jnp.asarray(_bucket_tiles())
    blk = MOBA_BLOCK
    return pl.pallas_call(
        _relbias_kernel,
        grid=(n_heads,),
        in_specs=[pl.BlockSpec(memory_space=pltpu.SMEM),
                  pl.BlockSpec((2, blk, blk), lambda h: (0, 0, 0))],
        out_specs=pl.BlockSpec((1, 2, blk, blk), lambda h: (h, 0, 0, 0)),
        out_shape=jax.ShapeDtypeStruct((n_heads, 2, blk, blk), F32),
        compiler_params=_cparams(1),
        name="relbias",
    )(rel_table, bkt)


def _shift_time(cur, prev_rows, j):
    keep = (CHUNK_SLABS - j) * SUBLANES
    sub = lax.broadcasted_iota(jnp.int32, (SUBLANES, cur.shape[1]), 0)
    fixed = []
    for k in range(j):
        slab = cur[keep + k * SUBLANES:keep + (k + 1) * SUBLANES, :]
        fixed.append(jnp.where(sub == 0, prev_rows[k], pltpu.roll(slab, 1, axis=0)))
    return jnp.concatenate(fixed + [cur[:keep, :]], axis=0)


def _pool_kernel(x_ref, xh_ref, ng1_ref, sh1_ref, sc1_ref, gt1_ref, w_ref, ps_ref,
                 ng2_ref, sh2_ref, sc2_ref, xo_ref, ho_ref, hs_ref, *, tm, seq):
    i = pl.program_id(0)
    tiles_per_seq = seq // tm
    b = lax.div(i, tiles_per_seq)
    row0 = lax.rem(i, tiles_per_seq) * tm
    group = xo_ref.shape[1] // len(POOL_WINDOWS)

    def row(ref):
        return ref[0, pl.ds(b, 1), :]

    def normmod(xv, g, shift, scale):
        return _rms(xv) * (g * (1.0 + scale)) + shift

    def last_steps(ref, tail_row0):
        sub = lax.broadcasted_iota(jnp.int32, (SUBLANES, ref.shape[1]), 0)
        out = []
        for m0 in range(0, POOL_HALO, SUBLANES):
            acc = jnp.zeros((SUBLANES, ref.shape[1]), F32)
            for r in range(SUBLANES):
                r0 = tail_row0 + (m0 + r) * SUBLANES
                slab = ref[r0:r0 + SUBLANES, :]
                if r + 1 < SUBLANES:
                    slab = pltpu.roll(slab, r + 1, axis=0)
                acc = jnp.where(sub == r, slab, acc)
            out.append(acc)
        return jnp.concatenate(out, axis=0)

    g1, shift1, scale1 = ng1_ref[0], row(sh1_ref), row(sc1_ref)
    hs_ref[...] = normmod(x_ref[...], g1, shift1, scale1)
    halo = normmod(last_steps(xh_ref, 0), g1, shift1, scale1)
    halo = jnp.where(row0 == 0, 0.0, halo)

    stored = lax.broadcasted_iota(jnp.int32, (CHUNK, 1), 0)
    step = lax.rem(stored, SUBLANES) * CHUNK_SLABS + lax.div(stored, SUBLANES)
    gate1 = row(gt1_ref) * ps_ref[0]
    for c in range(tm // CHUNK):
        rows = slice(c * CHUNK, (c + 1) * CHUNK)
        before = halo if c == 0 else last_steps(hs_ref, c * CHUNK - POOL_HALO * SUBLANES)
        pos = row0 + c * CHUNK + step
        for g, win in enumerate(POOL_WINDOWS):
            cols = slice(g * group, (g + 1) * group)
            h0 = hs_ref[rows, cols]
            acc, prev, span = h0, before[:, cols], 1
            while span < win:
                prev_rows = [prev[POOL_HALO - span + k:POOL_HALO - span + k + 1, :] for k in range(span)]
                acc = acc + _shift_time(acc, prev_rows, span)
                prev = prev + pltpu.roll(prev, span, axis=0)
                span *= 2
            cnt = jnp.minimum(pos + 1, win).astype(F32)
            dlt = (acc / cnt - h0).astype(BF16)
            y = jnp.dot(dlt, w_ref[g], preferred_element_type=F32)
            xo_ref[rows, cols] = x_ref[rows, cols] + gate1[:, cols] * y

    ho_ref[...] = normmod(xo_ref[...], ng2_ref[0], row(sh2_ref), row(sc2_ref)).astype(BF16)


def _pool(xf, mod, norm_g3, pool_w_b, pool_scale3, layer, li, seq):
    m, d = xf.shape
    tm = 2 * CHUNK
    n_grp, grp, _ = pool_w_b.shape[1:]
    halo_rows = POOL_HALO * SUBLANES
    halo_blocks = tm // halo_rows
    assert all(w & (w - 1) == 0 and w <= POOL_HALO for w in POOL_WINDOWS)
    kern = functools.partial(_pool_kernel, tm=tm, seq=seq)
    return pl.pallas_call(
        kern,
        grid=(m // tm,),
        in_specs=[pl.BlockSpec((tm, d), lambda i: (i, 0)),
                  pl.BlockSpec((halo_rows, d), lambda i: (jnp.maximum(i * halo_blocks - 1, 0), 0)),
                  _row_spec(layer * 2, d),
                  _mod_spec(layer, 0, d), _mod_spec(layer, 1, d), _mod_spec(layer, 2, d),
                  pl.BlockSpec((None, n_grp, grp, grp), lambda i: (li, 0, 0, 0)),
                  _row_spec(li, d),
                  _row_spec(layer * 2 + 1, d),
                  _mod_spec(layer, 3, d), _mod_spec(layer, 4, d)],
        out_specs=[pl.BlockSpec((tm, d), lambda i: (i, 0)),
                   pl.BlockSpec((tm, d), lambda i: (i, 0))],
        out_shape=[jax.ShapeDtypeStruct((m, d), F32), jax.ShapeDtypeStruct((m, d), BF16)],
        scratch_shapes=[pltpu.VMEM((tm, d), F32)],
        compiler_params=_cparams(1),
        name=f"pool{layer}",
    )(xf, xf, norm_g3, mod, mod, mod, pool_w_b, pool_scale3, norm_g3, mod, mod)


def _mm_res_kernel(*refs, mode, tm, chunk, seq, cast):
    if cast:
        *refs, wb_ref = refs
    if mode == "x":
        a_ref, w_ref, x_ref, gt_ref, xo_ref = refs
    elif mode == "xh":
        a_ref, w_ref, x_ref, gt_ref, ng_ref, sh_ref, sc_ref, xo_ref, ho_ref = refs
    else:
        a_ref, w_ref, x_ref, gt_ref, ng_ref, fo_ref = refs
    if cast:
        _cast_rows(pl.program_id(0) == 0, [(w_ref, wb_ref)])
        w_ref = wb_ref
    b = lax.div(pl.program_id(0), seq // tm)
    gate = gt_ref[0, pl.ds(b, 1), :]
    for c in range(tm // chunk):
        rows = slice(c * chunk, (c + 1) * chunk)
        y = jnp.dot(a_ref[rows, :], w_ref[...], preferred_element_type=F32)
        xn = x_ref[rows, :] + gate * y
        if mode in ("x", "xh"):
            xo_ref[rows, :] = xn
        if mode == "xh":
            shift, scale = sh_ref[0, pl.ds(b, 1), :], sc_ref[0, pl.ds(b, 1), :]
            ho_ref[rows, :] = (_rms(xn) * (ng_ref[0] * (1.0 + scale)) + shift).astype(BF16)
        if mode == "final":
            fo_ref[rows, :] = _rms(xn) * ng_ref[0]


def _mm_res(a, w_b, wi, xf, mod, gate_at, seq, *, mode, norm=None, norm_idx=None, next_layer=None):
    m, kdim = a.shape
    d = xf.shape[1]
    tm, chunk = (2 * MM_ROWS, MM_ROWS) if kdim <= d else (MM_ROWS, MM_ROWS)
    cast = w_b.dtype != BF16
    kern = functools.partial(_mm_res_kernel, mode=mode, tm=tm, chunk=chunk, seq=seq, cast=cast)
    in_specs = [pl.BlockSpec((tm, kdim), lambda i: (i, 0)),
                pl.BlockSpec((None, kdim, d), lambda i: (wi, 0, 0), pipeline_mode=pl.Buffered(1)),
                pl.BlockSpec((tm, d), lambda i: (i, 0)),
                _mod_spec(gate_at[0], gate_at[1], d)]
    args = [a, w_b, xf, mod]
    row_out = pl.BlockSpec((tm, d), lambda i: (i, 0))
    if mode == "x":
        out_specs, out_shape = [row_out], [jax.ShapeDtypeStruct((m, d), F32)]
    elif mode == "xh":
        in_specs += [_row_spec(norm_idx, d), _mod_spec(next_layer[0], next_layer[1], d),
                     _mod_spec(next_layer[0], next_layer[1] + 1, d)]
        args += [norm, mod, mod]
        out_specs = [row_out, row_out]
        out_shape = [jax.ShapeDtypeStruct((m, d), F32), jax.ShapeDtypeStruct((m, d), BF16)]
    else:
        in_specs += [_row_spec(norm_idx, d)]
        args += [norm]
        out_specs, out_shape = [row_out], [jax.ShapeDtypeStruct((m, d), F32)]
    return pl.pallas_call(
        kern, grid=(m // tm,), in_specs=in_specs, out_specs=out_specs, out_shape=out_shape,
        scratch_shapes=[pltpu.VMEM((kdim, d), BF16)] if cast else [],
        compiler_params=_cparams(1), name=f"mmres_{mode}_{kdim}",
    )(*args)


def _causal_conv3(u, cw_ref, b_ref, carry):
    z0 = cw_ref[0, 0:1, :] * u
    z1 = cw_ref[0, 1:2, :] * u + _shift_time(z0, [carry[0]], 1)
    out = cw_ref[0, 2:3, :] * u + _shift_time(z1, [carry[1]], 1) + b_ref[0]
    return out, (z0[-1:, :], z1[-1:, :])


def _up_kernel(a_ref, wv_ref, wg_ref, cwv_ref, cwg_ref, bv_ref, bg_ref, wd_ref, o_ref, wdb_ref,
               wvb_ref, wgb_ref, *, rc):
    wdb_ref[...] = wd_ref[...].astype(BF16)
    _cast_rows(pl.program_id(1) == 0, [(wv_ref, wvb_ref), (wg_ref, wgb_ref)])

    seq, tf = o_ref.shape
    zero = jnp.zeros((1, tf), F32)
    carry_v = carry_g = (zero, zero)

    def project(c):
        a = a_ref[c * rc:(c + 1) * rc, :]
        ug = jnp.dot(a, wgb_ref[...], preferred_element_type=F32)
        uv = jnp.dot(a, wvb_ref[...], preferred_element_type=F32)
        return uv, ug

    nxt = project(0)
    for c in range(seq // rc):
        uv, ug = nxt
        if c + 1 < seq // rc:
            nxt = project(c + 1)
        val, carry_v = _causal_conv3(uv, cwv_ref, bv_ref, carry_v)
        gate, carry_g = _causal_conv3(ug, cwg_ref, bg_ref, carry_g)
        o_ref[c * rc:(c + 1) * rc, :] = (_silu(gate) * val).astype(BF16)


def _up(h, w_up, conv_w, conv_b3, w_down, layer, seq):
    m, d = h.shape
    f = w_up.shape[2] // 2
    tf = 512
    nf = f // tf
    nbat = m // seq
    wd_rows = f // (nf * nbat)
    assert wd_rows * nf * nbat == f and wd_rows % 16 == 0
    kern = functools.partial(_up_kernel, rc=CHUNK)
    return pl.pallas_call(
        kern,
        grid=(nf, nbat),
        in_specs=[pl.BlockSpec((seq, d), lambda j, b: (b, 0)),
                  pl.BlockSpec((None, d, tf), lambda j, b: (layer, 0, j)),
                  pl.BlockSpec((None, d, tf), lambda j, b: (layer, 0, nf + j)),
                  pl.BlockSpec((1, CONV_WIDTH, tf), lambda j, b: (layer, 0, j)),
                  pl.BlockSpec((1, CONV_WIDTH, tf), lambda j, b: (layer, 0, nf + j)),
                  pl.BlockSpec((1, 1, tf), lambda j, b: (layer, 0, j)),
                  pl.BlockSpec((1, 1, tf), lambda j, b: (layer, 0, nf + j)),
                  pl.BlockSpec((None, wd_rows, d), lambda j, b: (layer, j * nbat + b, 0))],
        out_specs=[pl.BlockSpec((seq, tf), lambda j, b: (b, j)),
                   pl.BlockSpec((None, wd_rows, d), lambda j, b: (0, j * nbat + b, 0))],
        out_shape=[jax.ShapeDtypeStruct((m, f), BF16), jax.ShapeDtypeStruct((1, f, d), BF16)],
        scratch_shapes=[pltpu.VMEM((d, tf), BF16), pltpu.VMEM((d, tf), BF16)],
        compiler_params=_cparams(2),
        name=f"up{layer}",
    )(h, w_up, w_up, conv_w, conv_w, conv_b3, conv_b3, w_down)


def _qkv_kernel(a_ref, w_ref, o_ref, wb_ref, *, d_model):
    _cast_rows(pl.program_id(1) == 0, [(w_ref, wb_ref)])
    y = jnp.dot(a_ref[...], wb_ref[...], preferred_element_type=F32)
    tn = o_ref.shape[1]
    scale = jnp.where(pl.program_id(0) * tn < d_model, HEAD_DIM ** -0.5 * LOG2E, 1.0)
    o_ref[...] = (y * scale).astype(BF16)


def _qkv(h, w_qkv, li):
    m, d = h.shape
    n = w_qkv.shape[2]
    tm, tn = 1024, 1024
    assert d % tn == 0
    return pl.pallas_call(
        functools.partial(_qkv_kernel, d_model=d),
        grid=(n // tn, m // tm),
        in_specs=[pl.BlockSpec((tm, d), lambda j, i: (i, 0)),
                  pl.BlockSpec((None, d, tn), lambda j, i: (li, 0, j))],
        out_specs=pl.BlockSpec((tm, tn), lambda j, i: (i, j)),
        out_shape=jax.ShapeDtypeStruct((m, n), BF16),
        scratch_shapes=[pltpu.VMEM((d, tn), BF16)],
        compiler_params=_cparams(2),
        name="qkv",
    )(h, w_qkv)


def _split_bf16(v):
    pieces, rest = [], v
    for _ in range(N_SPLIT):
        piece = rest.astype(BF16).astype(F32)
        pieces.append(piece)
        rest = rest - piece
    return pieces


def _key_features(seq):
    nb = seq // MOBA_BLOCK
    assert nb + N_SPLIT * nb <= HEAD_DIM
    blk_of = np.arange(seq) // MOBA_BLOCK
    kf = np.zeros((seq, HEAD_DIM), np.float32)
    for n in range(nb):
        kf[blk_of == n, n] = NEG
    for j in range(nb):
        kf[blk_of <= j - 2, nb + N_SPLIT * j:nb + N_SPLIT * (j + 1)] = 1.0
    return jnp.asarray(kf, dtype=BF16)


def _score_slot(j, n):
    return j * (j + 1) // 2 + n


def _attn_scores(tab_ref, head, q_ref, k_ref, v_ref, kf_ref, kx_ref, vx_ref, qx_ref, s_ref):
    blk = MOBA_BLOCK
    seq, hd = q_ref.shape
    nb = seq // blk
    nt = (((1,), (1,)), ((), ()))

    kx_ref[:, 0:hd] = k_ref[...]
    kx_ref[:, hd:] = kf_ref[...]
    vx_ref[:, 0:hd] = v_ref[...]
    vx_ref[:, hd:] = jnp.ones((seq, hd), BF16)
    qx_ref[:, 0:hd] = q_ref[...]

    lane = lax.broadcasted_iota(jnp.int32, (1, hd), 1)
    far_p = _split_bf16(jnp.full((1, hd), tab_ref[REL_BUCKETS - 1, head] * LOG2E, F32))
    kmean = jnp.sum(k_ref[...].astype(F32).reshape(nb, blk, hd), axis=1) * (1.0 / blk)
    ks = jnp.concatenate(_split_bf16(kmean) + [jnp.zeros((nb, hd), F32)], axis=0).astype(BF16)
    n_idx = lax.broadcasted_iota(jnp.int32, (nb, blk), 0)

    for j in range(nb):
        rows = slice(j * blk, (j + 1) * blk)
        if j < 2:
            qx_ref[rows, hd:] = jnp.zeros((blk, hd), BF16)
            continue
        farvec = jnp.zeros((1, hd), F32)
        for piece in range(N_SPLIT):
            farvec = jnp.where(lane == nb + N_SPLIT * j + piece, far_p[piece], farvec)
        if j > MOBA_TOPK:
            gt = lax.dot_general(ks, q_ref[rows, :], nt, preferred_element_type=F32)
            gate = gt[0:nb] + gt[nb:2 * nb] + gt[2 * nb:3 * nb]
            rank = jnp.zeros((nb, blk), jnp.int32)
            for mm in range(j):
                gm = gate[mm:mm + 1, :]
                beats = (gm > gate) | ((gm == gate) & (mm < n_idx))
                rank = rank + beats.astype(jnp.int32)
            drop_t = ((rank >= MOBA_TOPK) & (n_idx < j)).astype(F32)
            drop = jnp.concatenate([drop_t, jnp.zeros((hd - nb, blk), F32)], axis=0).T
            qx_ref[rows, hd:] = (drop + farvec).astype(BF16)
        else:
            qx_ref[rows, hd:] = jnp.broadcast_to(farvec, (blk, hd)).astype(BF16)

    for n in range(nb):
        s = lax.dot_general(qx_ref[n * blk:, :], kx_ref[n * blk:(n + 1) * blk, :], nt,
                            preferred_element_type=F32)
        for j in range(n, nb):
            s_ref[_score_slot(j, n)] = s[(j - n) * blk:(j - n + 1) * blk, :]


def _attn_outputs(bias_ref, vx_ref, s_ref, o_ref):
    blk = MOBA_BLOCK
    seq, hd = o_ref.shape
    for j in reversed(range(seq // blk)):
        logits = []
        for n in range(j + 1):
            lg = s_ref[_score_slot(j, n)]
            if n == j:
                lg = lg + bias_ref[1]
            elif n == j - 1:
                lg = lg + bias_ref[0]
            logits.append(lg)
        mx = jnp.max(functools.reduce(jnp.maximum, logits), axis=-1, keepdims=True)
        p = jnp.concatenate([jnp.exp2(lg - mx).astype(BF16) for lg in logits], axis=-1)
        acc = jnp.dot(p, vx_ref[0:(j + 1) * blk, :], preferred_element_type=F32)
        o_ref[j * blk:(j + 1) * blk, :] = (acc[:, 0:hd] / acc[:, hd:]).astype(BF16)


def _attn_kernel(tab_ref, q_ref, k_ref, v_ref, bias_ref, kf_ref, o_ref, kx_ref, vx_ref, qx_ref, s_ref):
    def cols(hh):
        return slice(hh * HEAD_DIM, (hh + 1) * HEAD_DIM)

    def scores(hh):
        _attn_scores(tab_ref, pl.program_id(1) * ATTN_HEADS_PER_STEP + hh, q_ref.at[:, cols(hh)],
                     k_ref.at[:, cols(hh)], v_ref.at[:, cols(hh)], kf_ref,
                     kx_ref.at[hh], vx_ref.at[hh], qx_ref.at[hh], s_ref.at[hh % 2])

    scores(0)
    for hh in range(ATTN_HEADS_PER_STEP):
        if hh + 1 < ATTN_HEADS_PER_STEP:
            scores(hh + 1)
        _attn_outputs(bias_ref.at[hh], vx_ref.at[hh], s_ref.at[hh % 2], o_ref.at[:, cols(hh)])


def _attn(qkv, bias, rel_table, seq):
    m, n3 = qkv.shape
    d = n3 // 3
    hps = ATTN_HEADS_PER_STEP
    groups = d // (hps * HEAD_DIM)
    blk = MOBA_BLOCK
    assert seq % blk == 0
    head_cols = pl.BlockSpec((seq, hps * HEAD_DIM), lambda b, g: (b, g))
    return pl.pallas_call(
        _attn_kernel,
        grid=(m // seq, groups),
        in_specs=[pl.BlockSpec(memory_space=pltpu.SMEM),
                  head_cols,
                  pl.BlockSpec((seq, hps * HEAD_DIM), lambda b, g: (b, groups + g)),
                  pl.BlockSpec((seq, hps * HEAD_DIM), lambda b, g: (b, 2 * groups + g)),
                  pl.BlockSpec((hps, 2, blk, blk), lambda b, g: (g, 0, 0, 0)),
                  pl.BlockSpec((seq, HEAD_DIM), lambda b, g: (0, 0))],
        out_specs=head_cols,
        out_shape=jax.ShapeDtypeStruct((m, d), BF16),
        scratch_shapes=[pltpu.VMEM((hps, seq, 2 * HEAD_DIM), BF16),
                        pltpu.VMEM((hps, seq, 2 * HEAD_DIM), BF16),
                        pltpu.VMEM((hps, seq, 2 * HEAD_DIM), BF16),
                        pltpu.VMEM((2, _score_slot(seq // blk, 0), blk, blk), F32)],
        compiler_params=_cparams(2),
        name="attn",
    )(rel_table, qkv, qkv, qkv, bias, _key_features(seq))


def kernel(x, c, norm_g, w_ada, b_ada, pool_w, pool_scale, w_qkv, w_o, rel_table,
           w_up, conv_w, conv_b, w_down, final_g):
    bsz, seq, d = x.shape
    depth = w_ada.shape[0]
    m = bsz * seq

    mod = _ada(c, w_ada, b_ada)
    bias = _relbias(rel_table)

    pool_w_b = pool_w.astype(BF16)
    norm_g3 = norm_g.reshape(depth * 2, 1, d)
    pool_scale3 = pool_scale.reshape(-1, 1, d)
    conv_b3 = conv_b.reshape(depth, 1, -1)
    final_g3 = final_g.reshape(1, 1, d)

    assert seq % CHUNK == 0
    xf = _to_chunk_order(x).reshape(m, d)
    h = None
    out = None
    for i in range(depth):
        li = i // N_MIXERS
        if i % N_MIXERS == 0:
            xf, h2 = _pool(xf, mod, norm_g3, pool_w_b, pool_scale3, i, li, seq)
        else:
            o = _attn(_qkv(h, w_qkv, li), bias, rel_table, seq)
            xf, h2 = _mm_res(o, w_o, li, xf, mod, (i, 2), seq, mode="xh",
                             norm=norm_g3, norm_idx=i * 2 + 1, next_layer=(i, 3))
        a, w_down_b = _up(h2, w_up, conv_w, conv_b3, w_down, i, seq)
        if i == depth - 1:
            (out,) = _mm_res(a, w_down_b, 0, xf, mod, (i, 5), seq, mode="final",
                             norm=final_g3, norm_idx=0)
        elif (i + 1) % N_MIXERS == 0:
            (xf,) = _mm_res(a, w_down_b, 0, xf, mod, (i, 5), seq, mode="x")
        else:
            xf, h = _mm_res(a, w_down_b, 0, xf, mod, (i, 5), seq, mode="xh",
                            norm=norm_g3, norm_idx=(i + 1) * 2, next_layer=(i + 1, 0))
    return _to_chunk_order(out.reshape(bsz, seq, d), inverse=True)
```

```python
import functools
import math

import numpy as np
import jax
import jax.numpy as jnp
from jax import lax
from jax.experimental import pallas as pl
from jax.experimental.pallas import tpu as pltpu

F32 = jnp.float32
BF16 = jnp.bfloat16

N_MIXERS = 2
POOL_WINDOWS = (2, 4, 8, 16)
HEAD_DIM = 128
MOBA_BLOCK = 256
MOBA_TOPK = 3
REL_BUCKETS = 32
REL_MAX_DIST = 128
CONV_WIDTH = 3
EPS = 1e-6
NEG = -1e30
LOG2E = math.log2(math.e)
N_SPLIT = 3
ATTN_HEADS_PER_STEP = 2
N_MOD = 6
SUBLANES = 8
MOD_ROWS = SUBLANES
POOL_HALO = 16
CHUNK = 256
CHUNK_SLABS = CHUNK // SUBLANES
CAST_ROWS = 256
MM_ROWS = 256
VMEM_LIMIT =56 * 1024 * 1024


def _cparams(n_axes):
    return pltpu.CompilerParams(dimension_semantics=("arbitrary",) * n_axes,
                                vmem_limit_bytes=VMEM_LIMIT)


def _silu(v):
    return v / (1.0 + jnp.exp2(v * (-LOG2E)))


def _rms(xv):
    return xv * lax.rsqrt(jnp.mean(xv * xv, axis=-1, keepdims=True) + EPS)


def _cast_rows(pred, pairs):
    n_rows = pairs[0][0].shape[0]
    assert all(src.shape[0] == n_rows for src, _ in pairs) and n_rows % CAST_ROWS == 0

    def body(r, carry):
        rows = pl.ds(pl.multiple_of(r * CAST_ROWS, CAST_ROWS), CAST_ROWS)
        for src, dst in pairs:
            dst[rows, :] = src[rows, :].astype(BF16)
        return carry

    lax.fori_loop(0, jnp.where(pred, n_rows // CAST_ROWS, 0), body, 0)


def _ada_kernel(tab_ref, c_ref, w_ref, b_ref, bkt_ref, o_ref, bias_ref, *, n_heads):
    cond = _silu(c_ref[...]).astype(BF16)
    o_ref[0] = jnp.dot(cond, w_ref[0].astype(BF16), preferred_element_type=F32) + b_ref[0]

    step = pl.program_id(0) * N_MOD + pl.program_id(1)

    @pl.when(step < n_heads)
    def _():
        for t in range(2):
            bk = bkt_ref[t]
            acc = jnp.full(bk.shape, NEG, F32)
            for b in range(REL_BUCKETS):
                acc = jnp.where(bk == b, tab_ref[b, step] * LOG2E, acc)
            bias_ref[0, t] = acc


def _ada(c, w_ada, b_ada, rel_table):
    depth, d, _ = w_ada.shape
    n_heads = rel_table.shape[1]
    blk = MOBA_BLOCK
    assert n_heads <= depth * N_MOD
    cp = jnp.pad(c, ((0, MOD_ROWS - c.shape[0]), (0, 0)))
    b3 = b_ada.reshape(depth * N_MOD, 1, d)
    bkt = jnp.asarray(_bucket_tiles())
    return pl.pallas_call(
        functools.partial(_ada_kernel, n_heads=n_heads),
        grid=(depth, N_MOD),
        in_specs=[pl.BlockSpec(memory_space=pltpu.SMEM),
                  pl.BlockSpec((MOD_ROWS, d), lambda i, k: (0, 0)),
                  pl.BlockSpec((1, d, d), lambda i, k: (i, 0, k)),
                  pl.BlockSpec((1, 1, d), lambda i, k: (i * N_MOD + k, 0, 0)),
                  pl.BlockSpec((2, blk, blk), lambda i, k: (0, 0, 0))],
        out_specs=[pl.BlockSpec((1, MOD_ROWS, d), lambda i, k: (i * N_MOD + k, 0, 0)),
                   pl.BlockSpec((1, 2, blk, blk), lambda i, k: (jnp.minimum(i * N_MOD + k, n_heads - 1), 0, 0, 0))],
        out_shape=[jax.ShapeDtypeStruct((depth * N_MOD, MOD_ROWS, d), F32),
                   jax.ShapeDtypeStruct((n_heads, 2, blk, blk), F32)],
        compiler_params=_cparams(2),
        name="ada",
    )(rel_table, cp, w_ada, b3, bkt)


def _mod_spec(layer, k, d):
    idx = layer * N_MOD + k
    return pl.BlockSpec((1, MOD_ROWS, d), lambda *_: (idx, 0, 0))


def _row_spec(idx, d):
    return pl.BlockSpec((1, 1, d), lambda *_: (idx, 0, 0))


def _rel_bucket_np(dist):
    n = np.maximum(dist, 0)
    max_exact = REL_BUCKETS // 2
    nf = np.maximum(n, 1).astype(np.float32)
    large = max_exact + (np.log(nf / np.float32(max_exact)) / np.float32(np.log(REL_MAX_DIST / max_exact))
                         * np.float32(REL_BUCKETS - max_exact)).astype(np.int32)
    large = np.minimum(large, REL_BUCKETS - 1)
    return np.where(n < max_exact, n, large).astype(np.int32)


def _chunk_step_of_row():
    row = np.arange(CHUNK)
    return (row % SUBLANES) * CHUNK_SLABS + row // SUBLANES


def _to_chunk_order(x3, inverse=False):
    bsz, seq, d = x3.shape
    a, b = (CHUNK_SLABS, SUBLANES) if inverse else (SUBLANES, CHUNK_SLABS)
    return x3.reshape(bsz, seq // CHUNK, a, b, d).transpose(0, 1, 3, 2, 4).reshape(bsz, seq, d)


def _bucket_tiles():
    assert MOBA_BLOCK == CHUNK
    step = _chunk_step_of_row()
    t = step[:, None]
    s = step[None, :]
    prev = _rel_bucket_np(MOBA_BLOCK + t - s)
    own = np.where(t - s >= 0, _rel_bucket_np(t - s), -1)
    assert _rel_bucket_np(np.arange(MOBA_BLOCK + 1, 16 * MOBA_BLOCK)).min() == REL_BUCKETS - 1
    return np.stack([prev, own]).astype(np.int32)


def _shift_time(cur, prev_rows, j):
    keep = (CHUNK_SLABS - j) * SUBLANES
    sub = lax.broadcasted_iota(jnp.int32, (SUBLANES, cur.shape[1]), 0)
    fixed = []
    for k in range(j):
        slab = cur[keep + k * SUBLANES:keep + (k + 1) * SUBLANES, :]
        fixed.append(jnp.where(sub == 0, prev_rows[k], pltpu.roll(slab, 1, axis=0)))
    return jnp.concatenate(fixed + [cur[:keep, :]], axis=0)


def _pool_kernel(x_ref, xh_ref, ng1_ref, sh1_ref, sc1_ref, gt1_ref, w_ref, ps_ref,
                 ng2_ref, sh2_ref, sc2_ref, xo_ref, ho_ref, hs_ref, wb_ref, *, tm, seq):
    i = pl.program_id(0)
    _cast_rows(i == 0, [(w_ref.at[g], wb_ref.at[g]) for g in range(len(POOL_WINDOWS))])
    tiles_per_seq = seq // tm
    b = lax.div(i, tiles_per_seq)
    row0 = lax.rem(i, tiles_per_seq) * tm
    group = xo_ref.shape[1] // len(POOL_WINDOWS)

    def row(ref):
        return ref[0, pl.ds(b, 1), :]

    def normmod(xv, g, shift, scale):
        return _rms(xv) * (g * (1.0 + scale)) + shift

    def last_steps(ref, tail_row0):
        sub = lax.broadcasted_iota(jnp.int32, (SUBLANES, ref.shape[1]), 0)
        out = []
        for m0 in range(0, POOL_HALO, SUBLANES):
            acc = jnp.zeros((SUBLANES, ref.shape[1]), F32)
            for r in range(SUBLANES):
                r0 = tail_row0 + (m0 + r) * SUBLANES
                slab = ref[r0:r0 + SUBLANES, :]
                if r + 1 < SUBLANES:
                    slab = pltpu.roll(slab, r + 1, axis=0)
                acc = jnp.where(sub == r, slab, acc)
            out.append(acc)
        return jnp.concatenate(out, axis=0)

    g1, shift1, scale1 = ng1_ref[0], row(sh1_ref), row(sc1_ref)
    hs_ref[...] = normmod(x_ref[...], g1, shift1, scale1)
    halo = normmod(last_steps(xh_ref, 0), g1, shift1, scale1)
    halo = jnp.where(row0 == 0, 0.0, halo)

    stored = lax.broadcasted_iota(jnp.int32, (CHUNK, 1), 0)
    step = lax.rem(stored, SUBLANES) * CHUNK_SLABS + lax.div(stored, SUBLANES)
    gate1 = row(gt1_ref) * ps_ref[0]
    for c in range(tm // CHUNK):
        rows = slice(c * CHUNK, (c + 1) * CHUNK)
        before = halo if c == 0 else last_steps(hs_ref, c * CHUNK - POOL_HALO * SUBLANES)
        pos = row0 + c * CHUNK + step
        for g, win in enumerate(POOL_WINDOWS):
            cols = slice(g * group, (g + 1) * group)
            h0 = hs_ref[rows, cols]
            acc, prev, span = h0, before[:, cols], 1
            while span < win:
                prev_rows = [prev[POOL_HALO - span + k:POOL_HALO - span + k + 1, :] for k in range(span)]
                acc = acc + _shift_time(acc, prev_rows, span)
                prev = prev + pltpu.roll(prev, span, axis=0)
                span *= 2
            cnt = jnp.minimum(pos + 1, win).astype(F32)
            dlt = (acc / cnt - h0).astype(BF16)
            y = jnp.dot(dlt, wb_ref[g], preferred_element_type=F32)
            xo_ref[rows, cols] = x_ref[rows, cols] + gate1[:, cols] * y

    ho_ref[...] = normmod(xo_ref[...], ng2_ref[0], row(sh2_ref), row(sc2_ref)).astype(BF16)


def _pool(xf, mod, norm_g3, pool_w, pool_scale3, layer, li, seq):
    m, d = xf.shape
    tm = 2 * CHUNK
    n_grp, grp, _ = pool_w.shape[1:]
    halo_rows = POOL_HALO * SUBLANES
    halo_blocks = tm // halo_rows
    assert all(w & (w - 1) == 0 and w <= POOL_HALO for w in POOL_WINDOWS)
    kern = functools.partial(_pool_kernel, tm=tm, seq=seq)
    return pl.pallas_call(
        kern,
        grid=(m // tm,),
        in_specs=[pl.BlockSpec((tm, d), lambda i: (i, 0)),
                  pl.BlockSpec((halo_rows, d), lambda i: (jnp.maximum(i * halo_blocks - 1, 0), 0)),
                  _row_spec(layer * 2, d),
                  _mod_spec(layer, 0, d), _mod_spec(layer, 1, d), _mod_spec(layer, 2, d),
                  pl.BlockSpec((None, n_grp, grp, grp), lambda i: (li, 0, 0, 0), pipeline_mode=pl.Buffered(1)),
                  _row_spec(li, d),
                  _row_spec(layer * 2 + 1, d),
                  _mod_spec(layer, 3, d), _mod_spec(layer, 4, d)],
        out_specs=[pl.BlockSpec((tm, d), lambda i: (i, 0)),
                   pl.BlockSpec((tm, d), lambda i: (i, 0))],
        out_shape=[jax.ShapeDtypeStruct((m, d), F32), jax.ShapeDtypeStruct((m, d), BF16)],
        scratch_shapes=[pltpu.VMEM((tm, d), F32), pltpu.VMEM((n_grp, grp, grp), BF16)],
        compiler_params=_cparams(1),
        name=f"pool{layer}",
    )(xf, xf, norm_g3, mod, mod, mod, pool_w, pool_scale3, norm_g3, mod, mod)


def _mm_res_kernel(*refs, mode, tm, chunk, seq, cast):
    if cast:
        *refs, wb_ref = refs
    if mode == "x":
        a_ref, w_ref, x_ref, gt_ref, xo_ref = refs
    elif mode == "xh":
        a_ref, w_ref, x_ref, gt_ref, ng_ref, sh_ref, sc_ref, xo_ref, ho_ref = refs
    else:
        a_ref, w_ref, x_ref, gt_ref, ng_ref, fo_ref = refs
    if cast:
        _cast_rows(pl.program_id(0) == 0, [(w_ref, wb_ref)])
        w_ref = wb_ref
    b = lax.div(pl.program_id(0), seq // tm)
    gate = gt_ref[0, pl.ds(b, 1), :]
    for c in range(tm // chunk):
        rows = slice(c * chunk, (c + 1) * chunk)
        y = jnp.dot(a_ref[rows, :], w_ref[...], preferred_element_type=F32)
        xn = x_ref[rows, :] + gate * y
        if mode in ("x", "xh"):
            xo_ref[rows, :] = xn
        if mode == "xh":
            shift, scale = sh_ref[0, pl.ds(b, 1), :], sc_ref[0, pl.ds(b, 1), :]
            ho_ref[rows, :] = (_rms(xn) * (ng_ref[0] * (1.0 + scale)) + shift).astype(BF16)
        if mode == "final":
            fo_ref[rows, :] = _rms(xn) * ng_ref[0]


def _mm_res(a, w_b, wi, xf, mod, gate_at, seq, *, mode, norm=None, norm_idx=None, next_layer=None):
    m, kdim = a.shape
    d = xf.shape[1]
    tm, chunk = (2 * MM_ROWS, MM_ROWS) if kdim <= d else (MM_ROWS, MM_ROWS)
    cast = w_b.dtype != BF16
    kern = functools.partial(_mm_res_kernel, mode=mode, tm=tm, chunk=chunk, seq=seq, cast=cast)
    in_specs = [pl.BlockSpec((tm, kdim), lambda i: (i, 0)),
                pl.BlockSpec((None, kdim, d), lambda i: (wi, 0, 0), pipeline_mode=pl.Buffered(1)),
                pl.BlockSpec((tm, d), lambda i: (i, 0)),
                _mod_spec(gate_at[0], gate_at[1], d)]
    args = [a, w_b, xf, mod]
    row_out = pl.BlockSpec((tm, d), lambda i: (i, 0))
    if mode == "x":
        out_specs, out_shape = [row_out], [jax.ShapeDtypeStruct((m, d), F32)]
    elif mode == "xh":
        in_specs += [_row_spec(norm_idx, d), _mod_spec(next_layer[0], next_layer[1], d),
                     _mod_spec(next_layer[0], next_layer[1] + 1, d)]
        args += [norm, mod, mod]
        out_specs = [row_out, row_out]
        out_shape = [jax.ShapeDtypeStruct((m, d), F32), jax.ShapeDtypeStruct((m, d), BF16)]
    else:
        in_specs += [_row_spec(norm_idx, d)]
        args += [norm]
        out_specs, out_shape = [row_out], [jax.ShapeDtypeStruct((m, d), F32)]
    return pl.pallas_call(
        kern, grid=(m // tm,), in_specs=in_specs, out_specs=out_specs, out_shape=out_shape,
        scratch_shapes=[pltpu.VMEM((kdim, d), BF16)] if cast else [],
        compiler_params=_cparams(1), name=f"mmres_{mode}_{kdim}",
    )(*args)


def _causal_conv3(u, cw_ref, b_ref, carry):
    z0 = cw_ref[0, 0:1, :] * u
    z1 = cw_ref[0, 1:2, :] * u + _shift_time(z0, [carry[0]], 1)
    out = cw_ref[0, 2:3, :] * u + _shift_time(z1, [carry[1]], 1) + b_ref[0]
    return out, (z0[-1:, :], z1[-1:, :])


def _up_kernel(a_ref, wv_ref, wg_ref, cwv_ref, cwg_ref, bv_ref, bg_ref, wd_ref, o_ref, wdb_ref,
               wvb_ref, wgb_ref, *, rc):
    wdb_ref[...] = wd_ref[...].astype(BF16)
    _cast_rows(pl.program_id(1) == 0, [(wv_ref, wvb_ref), (wg_ref, wgb_ref)])

    seq, tf = o_ref.shape
    zero = jnp.zeros((1, tf), F32)
    carry_v = carry_g = (zero, zero)

    def project(c):
        a = a_ref[c * rc:(c + 1) * rc, :]
        ug = jnp.dot(a, wgb_ref[...], preferred_element_type=F32)
        uv = jnp.dot(a, wvb_ref[...], preferred_element_type=F32)
        return uv, ug

    nxt = project(0)
    for c in range(seq // rc):
        uv, ug = nxt
        if c + 1 < seq // rc:
            nxt = project(c + 1)
        val, carry_v = _causal_conv3(uv, cwv_ref, bv_ref, carry_v)
        gate, carry_g = _causal_conv3(ug, cwg_ref, bg_ref, carry_g)
        o_ref[c * rc:(c + 1) * rc, :] = (_silu(gate) * val).astype(BF16)


def _up(h, w_up, conv_w, conv_b3, w_down, layer, seq):
    m, d = h.shape
    f = w_up.shape[2] // 2
    tf = 512
    nf = f // tf
    nbat = m // seq
    wd_rows = f // (nf * nbat)
    assert wd_rows * nf * nbat == f and wd_rows % 16 == 0
    kern = functools.partial(_up_kernel, rc=CHUNK)
    return pl.pallas_call(
        kern,
        grid=(nf, nbat),
        in_specs=[pl.BlockSpec((seq, d), lambda j, b: (b, 0)),
                  pl.BlockSpec((None, d, tf), lambda j, b: (layer, 0, j)),
                  pl.BlockSpec((None, d, tf), lambda j, b: (layer, 0, nf + j)),
                  pl.BlockSpec((1, CONV_WIDTH, tf), lambda j, b: (layer, 0, j)),
                  pl.BlockSpec((1, CONV_WIDTH, tf), lambda j, b: (layer, 0, nf + j)),
                  pl.BlockSpec((1, 1, tf), lambda j, b: (layer, 0, j)),
                  pl.BlockSpec((1, 1, tf), lambda j, b: (layer, 0, nf + j)),
                  pl.BlockSpec((None, wd_rows, d), lambda j, b: (layer, j * nbat + b, 0))],
        out_specs=[pl.BlockSpec((seq, tf), lambda j, b: (b, j)),
                   pl.BlockSpec((None, wd_rows, d), lambda j, b: (0, j * nbat + b, 0))],
        out_shape=[jax.ShapeDtypeStruct((m, f), BF16), jax.ShapeDtypeStruct((1, f, d), BF16)],
        scratch_shapes=[pltpu.VMEM((d, tf), BF16), pltpu.VMEM((d, tf), BF16)],
        compiler_params=_cparams(2),
        name=f"up{layer}",
    )(h, w_up, w_up, conv_w, conv_w, conv_b3, conv_b3, w_down)


def _qkv_kernel(a_ref, w_ref, o_ref, wb_ref, *, d_model):
    _cast_rows(pl.program_id(1) == 0, [(w_ref, wb_ref)])
    y = jnp.dot(a_ref[...], wb_ref[...], preferred_element_type=F32)
    tn = o_ref.shape[1]
    scale = jnp.where(pl.program_id(0) * tn < d_model, HEAD_DIM ** -0.5 * LOG2E, 1.0)
    o_ref[...] = (y * scale).astype(BF16)


def _qkv(h, w_qkv, li):
    m, d = h.shape
    n = w_qkv.shape[2]
    tm, tn = 1024, 1024
    assert d % tn == 0
    return pl.pallas_call(
        functools.partial(_qkv_kernel, d_model=d),
        grid=(n // tn, m // tm),
        in_specs=[pl.BlockSpec((tm, d), lambda j, i: (i, 0)),
                  pl.BlockSpec((None, d, tn), lambda j, i: (li, 0, j))],
        out_specs=pl.BlockSpec((tm, tn), lambda j, i: (i, j)),
        out_shape=jax.ShapeDtypeStruct((m, n), BF16),
        scratch_shapes=[pltpu.VMEM((d, tn), BF16)],
        compiler_params=_cparams(2),
        name="qkv",
    )(h, w_qkv)


def _split_bf16(v):
    pieces, rest = [], v
    for _ in range(N_SPLIT):
        piece = rest.astype(BF16).astype(F32)
        pieces.append(piece)
        rest = rest - piece
    return pieces


def _key_features(seq):
    nb = seq // MOBA_BLOCK
    assert nb + N_SPLIT * nb <= HEAD_DIM
    blk_of = np.arange(seq) // MOBA_BLOCK
    kf = np.zeros((seq, HEAD_DIM), np.float32)
    for n in range(nb):
        kf[blk_of == n, n] = NEG
    for j in range(nb):
        kf[blk_of <= j - 2, nb + N_SPLIT * j:nb + N_SPLIT * (j + 1)] = 1.0
    return jnp.asarray(kf, dtype=BF16)


def _score_slot(j, n):
    return j * (j + 1) // 2 + n


def _attn_scores(tab_ref, head, q_ref, k_ref, v_ref, kf_ref, kx_ref, vx_ref, qx_ref, s_ref):
    blk = MOBA_BLOCK
    seq, hd = q_ref.shape
    nb = seq // blk
    nt = (((1,), (1,)), ((), ()))

    kx_ref[:, 0:hd] = k_ref[...]
    kx_ref[:, hd:] = kf_ref[...]
    vx_ref[:, 0:hd] = v_ref[...]
    vx_ref[:, hd:] = jnp.ones((seq, hd), BF16)
    qx_ref[:, 0:hd] = q_ref[...]

    lane = lax.broadcasted_iota(jnp.int32, (1, hd), 1)
    far_p = _split_bf16(jnp.full((1, hd), tab_ref[REL_BUCKETS - 1, head] * LOG2E, F32))
    kmean = jnp.sum(k_ref[...].astype(F32).reshape(nb, blk, hd), axis=1) * (1.0 / blk)
    ks = jnp.concatenate(_split_bf16(kmean) + [jnp.zeros((nb, hd), F32)], axis=0).astype(BF16)
    n_idx = lax.broadcasted_iota(jnp.int32, (nb, blk), 0)

    for j in range(nb):
        rows = slice(j * blk, (j + 1) * blk)
        if j < 2:
            qx_ref[rows, hd:] = jnp.zeros((blk, hd), BF16)
            continue
        farvec = jnp.zeros((1, hd), F32)
        for piece in range(N_SPLIT):
            farvec = jnp.where(lane == nb + N_SPLIT * j + piece, far_p[piece], farvec)
        if j > MOBA_TOPK:
            gt = lax.dot_general(ks, q_ref[rows, :], nt, preferred_element_type=F32)
            gate = gt[0:nb] + gt[nb:2 * nb] + gt[2 * nb:3 * nb]
            rank = jnp.zeros((nb, blk), jnp.int32)
            for mm in range(j):
                gm = gate[mm:mm + 1, :]
                beats = (gm > gate) | ((gm == gate) & (mm < n_idx))
                rank = rank + beats.astype(jnp.int32)
            drop_t = ((rank >= MOBA_TOPK) & (n_idx < j)).astype(F32)
            drop = jnp.concatenate([drop_t, jnp.zeros((hd - nb, blk), F32)], axis=0).T
            qx_ref[rows, hd:] = (drop + farvec).astype(BF16)
        else:
            qx_ref[rows, hd:] = jnp.broadcast_to(farvec, (blk, hd)).astype(BF16)

    for n in range(nb):
        s = lax.dot_general(qx_ref[n * blk:, :], kx_ref[n * blk:(n + 1) * blk, :], nt,
                            preferred_element_type=F32)
        for j in range(n, nb):
            s_ref[_score_slot(j, n)] = s[(j - n) * blk:(j - n + 1) * blk, :]


def _attn_outputs(bias_ref, vx_ref, s_ref, o_ref):
    blk = MOBA_BLOCK
    seq, hd = o_ref.shape
    for j in reversed(range(seq // blk)):
        logits = []
        for n in range(j + 1):
            lg = s_ref[_score_slot(j, n)]
            if n == j:
                lg = lg + bias_ref[1]
            elif n == j - 1:
                lg = lg + bias_ref[0]
            logits.append(lg)
        mx = jnp.max(functools.reduce(jnp.maximum, logits), axis=-1, keepdims=True)
        p = jnp.concatenate([jnp.exp2(lg - mx).astype(BF16) for lg in logits], axis=-1)
        acc = jnp.dot(p, vx_ref[0:(j + 1) * blk, :], preferred_element_type=F32)
        o_ref[j * blk:(j + 1) * blk, :] = (acc[:, 0:hd] / acc[:, hd:]).astype(BF16)


def _attn_kernel(tab_ref, q_ref, k_ref, v_ref, bias_ref, kf_ref, o_ref, kx_ref, vx_ref, qx_ref, s_ref):
    def cols(hh):
        return slice(hh * HEAD_DIM, (hh + 1) * HEAD_DIM)

    def scores(hh):
        _attn_scores(tab_ref, pl.program_id(1) * ATTN_HEADS_PER_STEP + hh, q_ref.at[:, cols(hh)],
                     k_ref.at[:, cols(hh)], v_ref.at[:, cols(hh)], kf_ref,
                     kx_ref.at[hh], vx_ref.at[hh], qx_ref.at[hh], s_ref.at[hh % 2])

    scores(0)
    for hh in range(ATTN_HEADS_PER_STEP):
        if hh + 1 < ATTN_HEADS_PER_STEP:
            scores(hh + 1)
        _attn_outputs(bias_ref.at[hh], vx_ref.at[hh], s_ref.at[hh % 2], o_ref.at[:, cols(hh)])


def _attn(qkv, bias, rel_table, seq):
    m, n3 = qkv.shape
    d = n3 // 3
    hps = ATTN_HEADS_PER_STEP
    groups = d // (hps * HEAD_DIM)
    blk = MOBA_BLOCK
    assert seq % blk == 0
    head_cols = pl.BlockSpec((seq, hps * HEAD_DIM), lambda b, g: (b, g))
    return pl.pallas_call(
        _attn_kernel,
        grid=(m // seq, groups),
        in_specs=[pl.BlockSpec(memory_space=pltpu.SMEM),
                  head_cols,
                  pl.BlockSpec((seq, hps * HEAD_DIM), lambda b, g: (b, groups + g)),
                  pl.BlockSpec((seq, hps * HEAD_DIM), lambda b, g: (b, 2 * groups + g)),
                  pl.BlockSpec((hps, 2, blk, blk), lambda b, g: (g, 0, 0, 0)),
                  pl.BlockSpec((seq, HEAD_DIM), lambda b, g: (0, 0))],
        out_specs=head_cols,
        out_shape=jax.ShapeDtypeStruct((m, d), BF16),
        scratch_shapes=[pltpu.VMEM((hps, seq, 2 * HEAD_DIM), BF16),
                        pltpu.VMEM((hps, seq, 2 * HEAD_DIM), BF16),
                        pltpu.VMEM((hps, seq, 2 * HEAD_DIM), BF16),
                        pltpu.VMEM((2, _score_slot(seq // blk, 0), blk, blk), F32)],
        compiler_params=_cparams(2),
        name="attn",
    )(rel_table, qkv, qkv, qkv, bias, _key_features(seq))


def kernel(x, c, norm_g, w_ada, b_ada, pool_w, pool_scale, w_qkv, w_o, rel_table,
           w_up, conv_w, conv_b, w_down, final_g):
    bsz, seq, d = x.shape
    depth = w_ada.shape[0]
    m = bsz * seq

    mod, bias = _ada(c, w_ada, b_ada, rel_table)

    norm_g3 = norm_g.reshape(depth * 2, 1, d)
    pool_scale3 = pool_scale.reshape(-1, 1, d)
    conv_b3 = conv_b.reshape(depth, 1, -1)
    final_g3 = final_g.reshape(1, 1, d)

    assert seq % CHUNK == 0
    xf = _to_chunk_order(x).reshape(m, d)
    h = None
    out = None
    for i in range(depth):
        li = i // N_MIXERS
        if i % N_MIXERS == 0:
            xf, h2 = _pool(xf, mod, norm_g3, pool_w, pool_scale3, i, li, seq)
        else:
            o = _attn(_qkv(h, w_qkv, li), bias, rel_table, seq)
            xf, h2 = _mm_res(o, w_o, li, xf, mod, (i, 2), seq, mode="xh",
                             norm=norm_g3, norm_idx=i * 2 + 1, next_layer=(i, 3))
        a, w_down_b = _up(h2, w_up, conv_w, conv_b3, w_down, i, seq)
        if i == depth - 1:
            (out,) = _mm_res(a, w_down_b, 0, xf, mod, (i, 5), seq, mode="final",
                             norm=final_g3, norm_idx=0)
        elif (i + 1) % N_MIXERS == 0:
            (xf,) = _mm_res(a, w_down_b, 0, xf, mod, (i, 5), seq, mode="x")
        else:
            xf, h = _mm_res(a, w_down_b, 0, xf, mod, (i, 5), seq, mode="xh",
                            norm=norm_g3, norm_idx=(i + 1) * 2, next_layer=(i + 1, 0))
    return _to_chunk_order(out.reshape(bsz, seq, d), inverse=True)
```

```python
import functools
import math

import numpy as np
import jax
import jax.numpy as jnp
from jax import lax
from jax.experimental import pallas as pl
from jax.experimental.pallas import tpu as pltpu

F32 = jnp.float32
BF16 = jnp.bfloat16

N_MIXERS = 2
POOL_WINDOWS = (2, 4, 8, 16)
HEAD_DIM = 128
MOBA_BLOCK = 256
MOBA_TOPK = 3
REL_BUCKETS = 32
REL_MAX_DIST = 128
CONV_WIDTH = 3
EPS = 1e-6
NEG = -1e30
LOG2E = math.log2(math.e)
N_SPLIT = 3
ATTN_HEADS_PER_STEP = 2
N_MOD = 6
SUBLANES = 8
MOD_ROWS = SUBLANES
POOL_HALO = 16
CHUNK = 256
CHUNK_SLABS = CHUNK // SUBLANES
CAST_ROWS = 256
MM_ROWS = 256
VMEM_LIMIT =56 * 1024 * 1024


def _cparams(n_axes):
    return pltpu.CompilerParams(dimension_semantics=("arbitrary",) * n_axes,
                                vmem_limit_bytes=VMEM_LIMIT)


def _silu(v):
    return v / (1.0 + jnp.exp2(v * (-LOG2E)))


def _rms(xv):
    return xv * lax.rsqrt(jnp.mean(xv * xv, axis=-1, keepdims=True) + EPS)


def _cast_rows(pred, pairs):
    n_rows = pairs[0][0].shape[0]
    assert all(src.shape[0] == n_rows for src, _ in pairs) and n_rows % CAST_ROWS == 0

    def body(r, carry):
        rows = pl.ds(pl.multiple_of(r * CAST_ROWS, CAST_ROWS), CAST_ROWS)
        for src, dst in pairs:
            dst[rows, :] = src[rows, :].astype(BF16)
        return carry

    lax.fori_loop(0, jnp.where(pred, n_rows // CAST_ROWS, 0), body, 0)


def _ada_kernel(c_ref, w_ref, b_ref, o_ref):
    cond = _silu(c_ref[...]).astype(BF16)
    o_ref[0] = jnp.dot(cond, w_ref[0].astype(BF16), preferred_element_type=F32) + b_ref[0]


def _ada(c, w_ada, b_ada):
    depth, d, _ = w_ada.shape
    cp = jnp.pad(c, ((0, MOD_ROWS - c.shape[0]), (0, 0)))
    b3 = b_ada.reshape(depth * N_MOD, 1, d)
    return pl.pallas_call(
        _ada_kernel,
        grid=(depth, N_MOD),
        in_specs=[pl.BlockSpec((MOD_ROWS, d), lambda i, k: (0, 0)),
                  pl.BlockSpec((1, d, d), lambda i, k: (i, 0, k)),
                  pl.BlockSpec((1, 1, d), lambda i, k: (i * N_MOD + k, 0, 0))],
        out_specs=pl.BlockSpec((1, MOD_ROWS, d), lambda i, k: (i * N_MOD + k, 0, 0)),
        out_shape=jax.ShapeDtypeStruct((depth * N_MOD, MOD_ROWS, d), F32),
        compiler_params=_cparams(2),
        name="ada",
    )(cp, w_ada, b3)


def _mod_spec(layer, k, d):
    idx = layer * N_MOD + k
    return pl.BlockSpec((1, MOD_ROWS, d), lambda *_: (idx, 0, 0))


def _row_spec(idx, d):
    return pl.BlockSpec((1, 1, d), lambda *_: (idx, 0, 0))


def _rel_bucket_np(dist):
    n = np.maximum(dist, 0)
    max_exact = REL_BUCKETS // 2
    nf = np.maximum(n, 1).astype(np.float32)
    large = max_exact + (np.log(nf / np.float32(max_exact)) / np.float32(np.log(REL_MAX_DIST / max_exact))
                         * np.float32(REL_BUCKETS - max_exact)).astype(np.int32)
    large = np.minimum(large, REL_BUCKETS - 1)
    return np.where(n < max_exact, n, large).astype(np.int32)


def _chunk_step_of_row():
    row = np.arange(CHUNK)
    return (row % SUBLANES) * CHUNK_SLABS + row // SUBLANES


def _to_chunk_order(x3, inverse=False):
    bsz, seq, d = x3.shape
    a, b = (CHUNK_SLABS, SUBLANES) if inverse else (SUBLANES, CHUNK_SLABS)
    return x3.reshape(bsz, seq // CHUNK, a, b, d).transpose(0, 1, 3, 2, 4).reshape(bsz, seq, d)


def _bucket_tiles():
    assert MOBA_BLOCK == CHUNK
    step = _chunk_step_of_row()
    t = step[:, None]
    s = step[None, :]
    prev = _rel_bucket_np(MOBA_BLOCK + t - s)
    own = np.where(t - s >= 0, _rel_bucket_np(t - s), -1)
    assert _rel_bucket_np(np.arange(MOBA_BLOCK + 1, 16 * MOBA_BLOCK)).min() == REL_BUCKETS - 1
    return np.stack([prev, own]).astype(np.int32)


def _relbias_kernel(tab_ref, bkt_ref, o_ref, *, buckets):
    h = pl.program_id(0)
    lanes = 128
    vals = [tab_ref[b, h] * LOG2E for b in range(REL_BUCKETS)]
    for t in range(buckets.shape[0]):
        for r in range(0, buckets.shape[1], SUBLANES):
            for c in range(0, buckets.shape[2], lanes):
                bk = bkt_ref[t, r:r + SUBLANES, c:c + lanes]
                acc = jnp.full(bk.shape, NEG, F32)
                for b in np.unique(buckets[t, r:r + SUBLANES, c:c + lanes]):
                    if b >= 0:
                        acc = jnp.where(bk == int(b), vals[int(b)], acc)
                o_ref[0, t, r:r + SUBLANES, c:c + lanes] = acc


def _relbias(rel_table):
    n_heads = rel_table.shape[1]
    buckets = _bucket_tiles()
    blk = MOBA_BLOCK
    return pl.pallas_call(
        functools.partial(_relbias_kernel, buckets=buckets),
        grid=(n_heads,),
        in_specs=[pl.BlockSpec(memory_space=pltpu.SMEM),
                  pl.BlockSpec((2, blk, blk), lambda h: (0, 0, 0))],
        out_specs=pl.BlockSpec((1, 2, blk, blk), lambda h: (h, 0, 0, 0)),
        out_shape=jax.ShapeDtypeStruct((n_heads, 2, blk, blk), F32),
        compiler_params=_cparams(1),
        name="relbias",
    )(rel_table, jnp.asarray(buckets))


def _shift_time(cur, prev_rows, j):
    keep = (CHUNK_SLABS - j) * SUBLANES
    sub = lax.broadcasted_iota(jnp.int32, (SUBLANES, cur.shape[1]), 0)
    fixed = []
    for k in range(j):
        slab = cur[keep + k * SUBLANES:keep + (k + 1) * SUBLANES, :]
        fixed.append(jnp.where(sub == 0, prev_rows[k], pltpu.roll(slab, 1, axis=0)))
    return jnp.concatenate(fixed + [cur[:keep, :]], axis=0)


def _pool_kernel(x_ref, xh_ref, ng1_ref, sh1_ref, sc1_ref, gt1_ref, w_ref, ps_ref,
                 ng2_ref, sh2_ref, sc2_ref, xo_ref, ho_ref, hs_ref, wb_ref, *, tm, seq):
    i = pl.program_id(0)
    _cast_rows(i == 0, [(w_ref.at[g], wb_ref.at[g]) for g in range(len(POOL_WINDOWS))])
    tiles_per_seq = seq // tm
    b = lax.div(i, tiles_per_seq)
    row0 = lax.rem(i, tiles_per_seq) * tm
    group = xo_ref.shape[1] // len(POOL_WINDOWS)

    def row(ref):
        return ref[0, pl.ds(b, 1), :]

    def normmod(xv, g, shift, scale):
        return _rms(xv) * (g * (1.0 + scale)) + shift

    def last_steps(ref, tail_row0):
        sub = lax.broadcasted_iota(jnp.int32, (SUBLANES, ref.shape[1]), 0)
        out = []
        for m0 in range(0, POOL_HALO, SUBLANES):
            acc = jnp.zeros((SUBLANES, ref.shape[1]), F32)
            for r in range(SUBLANES):
                r0 = tail_row0 + (m0 + r) * SUBLANES
                slab = ref[r0:r0 + SUBLANES, :]
                if r + 1 < SUBLANES:
                    slab = pltpu.roll(slab, r + 1, axis=0)
                acc = jnp.where(sub == r, slab, acc)
            out.append(acc)
        return jnp.concatenate(out, axis=0)

    g1, shift1, scale1 = ng1_ref[0], row(sh1_ref), row(sc1_ref)
    hs_ref[...] = normmod(x_ref[...], g1, shift1, scale1)
    halo = normmod(last_steps(xh_ref, 0), g1, shift1, scale1)
    halo = jnp.where(row0 == 0, 0.0, halo)

    stored = lax.broadcasted_iota(jnp.int32, (CHUNK, 1), 0)
    step = lax.rem(stored, SUBLANES) * CHUNK_SLABS + lax.div(stored, SUBLANES)
    gate1 = row(gt1_ref) * ps_ref[0]
    for c in range(tm // CHUNK):
        rows = slice(c * CHUNK, (c + 1) * CHUNK)
        before = halo if c == 0 else last_steps(hs_ref, c * CHUNK - POOL_HALO * SUBLANES)
        pos = row0 + c * CHUNK + step
        for g, win in enumerate(POOL_WINDOWS):
            cols = slice(g * group, (g + 1) * group)
            h0 = hs_ref[rows, cols]
            acc, prev, span = h0, before[:, cols], 1
            while span < win:
                prev_rows = [prev[POOL_HALO - span + k:POOL_HALO - span + k + 1, :] for k in range(span)]
                acc = acc + _shift_time(acc, prev_rows, span)
                prev = prev + pltpu.roll(prev, span, axis=0)
                span *= 2
            cnt = jnp.minimum(pos + 1, win).astype(F32)
            dlt = (acc / cnt - h0).astype(BF16)
            y = jnp.dot(dlt, wb_ref[g], preferred_element_type=F32)
            xo_ref[rows, cols] = x_ref[rows, cols] + gate1[:, cols] * y

    ho_ref[...] = normmod(xo_ref[...], ng2_ref[0], row(sh2_ref), row(sc2_ref)).astype(BF16)


def _pool(xf, mod, norm_g3, pool_w, pool_scale3, layer, li, seq):
    m, d = xf.shape
    tm = 2 * CHUNK
    n_grp, grp, _ = pool_w.shape[1:]
    halo_rows = POOL_HALO * SUBLANES
    halo_blocks = tm // halo_rows
    assert all(w & (w - 1) == 0 and w <= POOL_HALO for w in POOL_WINDOWS)
    kern = functools.partial(_pool_kernel, tm=tm, seq=seq)
    return pl.pallas_call(
        kern,
        grid=(m // tm,),
        in_specs=[pl.BlockSpec((tm, d), lambda i: (i, 0)),
                  pl.BlockSpec((halo_rows, d), lambda i: (jnp.maximum(i * halo_blocks - 1, 0), 0)),
                  _row_spec(layer * 2, d),
                  _mod_spec(layer, 0, d), _mod_spec(layer, 1, d), _mod_spec(layer, 2, d),
                  pl.BlockSpec((None, n_grp, grp, grp), lambda i: (li, 0, 0, 0), pipeline_mode=pl.Buffered(1)),
                  _row_spec(li, d),
                  _row_spec(layer * 2 + 1, d),
                  _mod_spec(layer, 3, d), _mod_spec(layer, 4, d)],
        out_specs=[pl.BlockSpec((tm, d), lambda i: (i, 0)),
                   pl.BlockSpec((tm, d), lambda i: (i, 0))],
        out_shape=[jax.ShapeDtypeStruct((m, d), F32), jax.ShapeDtypeStruct((m, d), BF16)],
        scratch_shapes=[pltpu.VMEM((tm, d), F32), pltpu.VMEM((n_grp, grp, grp), BF16)],
        compiler_params=_cparams(1),
        name=f"pool{layer}",
    )(xf, xf, norm_g3, mod, mod, mod, pool_w, pool_scale3, norm_g3, mod, mod)


def _mm_res_kernel(*refs, mode, tm, chunk, seq, cast):
    if cast:
        *refs, wb_ref = refs
    if mode == "x":
        a_ref, w_ref, x_ref, gt_ref, xo_ref = refs
    elif mode == "xh":
        a_ref, w_ref, x_ref, gt_ref, ng_ref, sh_ref, sc_ref, xo_ref, ho_ref = refs
    else:
        a_ref, w_ref, x_ref, gt_ref, ng_ref, fo_ref = refs
    if cast:
        _cast_rows(pl.program_id(0) == 0, [(w_ref, wb_ref)])
        w_ref = wb_ref
    b = lax.div(pl.program_id(0), seq // tm)
    gate = gt_ref[0, pl.ds(b, 1), :]
    for c in range(tm // chunk):
        rows = slice(c * chunk, (c + 1) * chunk)
        y = jnp.dot(a_ref[rows, :], w_ref[...], preferred_element_type=F32)
        xn = x_ref[rows, :] + gate * y
        if mode in ("x", "xh"):
            xo_ref[rows, :] = xn
        if mode == "xh":
            shift, scale = sh_ref[0, pl.ds(b, 1), :], sc_ref[0, pl.ds(b, 1), :]
            ho_ref[rows, :] = (_rms(xn) * (ng_ref[0] * (1.0 + scale)) + shift).astype(BF16)
        if mode == "final":
            fo_ref[rows, :] = _rms(xn) * ng_ref[0]


def _mm_res(a, w_b, wi, xf, mod, gate_at, seq, *, mode, norm=None, norm_idx=None, next_layer=None):
    m, kdim = a.shape
    d = xf.shape[1]
    tm, chunk = (2 * MM_ROWS, MM_ROWS) if kdim <= d else (MM_ROWS, MM_ROWS)
    cast = w_b.dtype != BF16
    kern = functools.partial(_mm_res_kernel, mode=mode, tm=tm, chunk=chunk, seq=seq, cast=cast)
    in_specs = [pl.BlockSpec((tm, kdim), lambda i: (i, 0)),
                pl.BlockSpec((None, kdim, d), lambda i: (wi, 0, 0), pipeline_mode=pl.Buffered(1)),
                pl.BlockSpec((tm, d), lambda i: (i, 0)),
                _mod_spec(gate_at[0], gate_at[1], d)]
    args = [a, w_b, xf, mod]
    row_out = pl.BlockSpec((tm, d), lambda i: (i, 0))
    if mode == "x":
        out_specs, out_shape = [row_out], [jax.ShapeDtypeStruct((m, d), F32)]
    elif mode == "xh":
        in_specs += [_row_spec(norm_idx, d), _mod_spec(next_layer[0], next_layer[1], d),
                     _mod_spec(next_layer[0], next_layer[1] + 1, d)]
        args += [norm, mod, mod]
        out_specs = [row_out, row_out]
        out_shape = [jax.ShapeDtypeStruct((m, d), F32), jax.ShapeDtypeStruct((m, d), BF16)]
    else:
        in_specs += [_row_spec(norm_idx, d)]
        args += [norm]
        out_specs, out_shape = [row_out], [jax.ShapeDtypeStruct((m, d), F32)]
    return pl.pallas_call(
        kern, grid=(m // tm,), in_specs=in_specs, out_specs=out_specs, out_shape=out_shape,
        scratch_shapes=[pltpu.VMEM((kdim, d), BF16)] if cast else [],
        compiler_params=_cparams(1), name=f"mmres_{mode}_{kdim}",
    )(*args)


def _causal_conv3(u, cw_ref, b_ref, carry):
    z0 = cw_ref[0, 0:1, :] * u
    z1 = cw_ref[0, 1:2, :] * u + _shift_time(z0, [carry[0]], 1)
    out = cw_ref[0, 2:3, :] * u + _shift_time(z1, [carry[1]], 1) + b_ref[0]
    return out, (z0[-1:, :], z1[-1:, :])


def _up_kernel(a_ref, wv_ref, wg_ref, cwv_ref, cwg_ref, bv_ref, bg_ref, wd_ref, o_ref, wdb_ref,
               wvb_ref, wgb_ref, *, rc):
    wdb_ref[...] = wd_ref[...].astype(BF16)
    _cast_rows(pl.program_id(1) == 0, [(wv_ref, wvb_ref), (wg_ref, wgb_ref)])

    seq, tf = o_ref.shape
    zero = jnp.zeros((1, tf), F32)
    carry_v = carry_g = (zero, zero)

    def project(c):
        a = a_ref[c * rc:(c + 1) * rc, :]
        ug = jnp.dot(a, wgb_ref[...], preferred_element_type=F32)
        uv = jnp.dot(a, wvb_ref[...], preferred_element_type=F32)
        return uv, ug

    nxt = project(0)
    for c in range(seq // rc):
        uv, ug = nxt
        if c + 1 < seq // rc:
            nxt = project(c + 1)
        val, carry_v = _causal_conv3(uv, cwv_ref, bv_ref, carry_v)
        gate, carry_g = _causal_conv3(ug, cwg_ref, bg_ref, carry_g)
        o_ref[c * rc:(c + 1) * rc, :] = (_silu(gate) * val).astype(BF16)


def _up(h, w_up, conv_w, conv_b3, w_down, layer, seq):
    m, d = h.shape
    f = w_up.shape[2] // 2
    tf = 512
    nf = f // tf
    nbat = m // seq
    wd_rows = f // (nf * nbat)
    assert wd_rows * nf * nbat == f and wd_rows % 16 == 0
    kern = functools.partial(_up_kernel, rc=CHUNK)
    return pl.pallas_call(
        kern,
        grid=(nf, nbat),
        in_specs=[pl.BlockSpec((seq, d), lambda j, b: (b, 0)),
                  pl.BlockSpec((None, d, tf), lambda j, b: (layer, 0, j)),
                  pl.BlockSpec((None, d, tf), lambda j, b: (layer, 0, nf + j)),
                  pl.BlockSpec((1, CONV_WIDTH, tf), lambda j, b: (layer, 0, j)),
                  pl.BlockSpec((1, CONV_WIDTH, tf), lambda j, b: (layer, 0, nf + j)),
                  pl.BlockSpec((1, 1, tf), lambda j, b: (layer, 0, j)),
                  pl.BlockSpec((1, 1, tf), lambda j, b: (layer, 0, nf + j)),
                  pl.BlockSpec((None, wd_rows, d), lambda j, b: (layer, j * nbat + b, 0))],
        out_specs=[pl.BlockSpec((seq, tf), lambda j, b: (b, j)),
                   pl.BlockSpec((None, wd_rows, d), lambda j, b: (0, j * nbat + b, 0))],
        out_shape=[jax.ShapeDtypeStruct((m, f), BF16), jax.ShapeDtypeStruct((1, f, d), BF16)],
        scratch_shapes=[pltpu.VMEM((d, tf), BF16), pltpu.VMEM((d, tf), BF16)],
        compiler_params=_cparams(2),
        name=f"up{layer}",
    )(h, w_up, w_up, conv_w, conv_w, conv_b3, conv_b3, w_down)


def _qkv_kernel(a_ref, w_ref, o_ref, wb_ref, *, d_model):
    _cast_rows(pl.program_id(1) == 0, [(w_ref, wb_ref)])
    y = jnp.dot(a_ref[...], wb_ref[...], preferred_element_type=F32)
    tn = o_ref.shape[1]
    scale = jnp.where(pl.program_id(0) * tn < d_model, HEAD_DIM ** -0.5 * LOG2E, 1.0)
    o_ref[...] = (y * scale).astype(BF16)


def _qkv(h, w_qkv, li):
    m, d = h.shape
    n = w_qkv.shape[2]
    tm, tn = 2048, 1024
    assert d % tn == 0 and m % tm == 0
    return pl.pallas_call(
        functools.partial(_qkv_kernel, d_model=d),
        grid=(n // tn, m // tm),
        in_specs=[pl.BlockSpec((tm, d), lambda j, i: (i, 0)),
                  pl.BlockSpec((None, d, tn), lambda j, i: (li, 0, j))],
        out_specs=pl.BlockSpec((tm, tn), lambda j, i: (i, j)),
        out_shape=jax.ShapeDtypeStruct((m, n), BF16),
        scratch_shapes=[pltpu.VMEM((d, tn), BF16)],
        compiler_params=_cparams(2),
        name="qkv",
    )(h, w_qkv)


def _split_bf16(v):
    pieces, rest = [], v
    for _ in range(N_SPLIT):
        piece = rest.astype(BF16).astype(F32)
        pieces.append(piece)
        rest = rest - piece
    return pieces


def _key_features(seq):
    nb = seq // MOBA_BLOCK
    assert nb + N_SPLIT * nb <= HEAD_DIM
    blk_of = np.arange(seq) // MOBA_BLOCK
    kf = np.zeros((seq, HEAD_DIM), np.float32)
    for n in range(nb):
        kf[blk_of == n, n] = NEG
    for j in range(nb):
        kf[blk_of <= j - 2, nb + N_SPLIT * j:nb + N_SPLIT * (j + 1)] = 1.0
    return jnp.asarray(kf, dtype=BF16)


def _score_slot(j, n):
    return j * (j + 1) // 2 + n


def _attn_scores(tab_ref, head, q_ref, k_ref, v_ref, kf_ref, kx_ref, vx_ref, qx_ref, s_ref):
    blk = MOBA_BLOCK
    seq, hd = q_ref.shape
    nb = seq // blk
    nt = (((1,), (1,)), ((), ()))

    kx_ref[:, 0:hd] = k_ref[...]
    kx_ref[:, hd:] = kf_ref[...]
    vx_ref[:, 0:hd] = v_ref[...]
    vx_ref[:, hd:] = jnp.ones((seq, hd), BF16)
    qx_ref[:, 0:hd] = q_ref[...]

    lane = lax.broadcasted_iota(jnp.int32, (1, hd), 1)
    far_p = _split_bf16(jnp.full((1, hd), tab_ref[REL_BUCKETS - 1, head] * LOG2E, F32))
    kmean = jnp.sum(k_ref[...].astype(F32).reshape(nb, blk, hd), axis=1) * (1.0 / blk)
    ks = jnp.concatenate(_split_bf16(kmean) + [jnp.zeros((nb, hd), F32)], axis=0).astype(BF16)
    n_idx = lax.broadcasted_iota(jnp.int32, (nb, blk), 0)

    for j in range(nb):
        rows = slice(j * blk, (j + 1) * blk)
        if j < 2:
            qx_ref[rows, hd:] = jnp.zeros((blk, hd), BF16)
            continue
        farvec = jnp.zeros((1, hd), F32)
        for piece in range(N_SPLIT):
            farvec = jnp.where(lane == nb + N_SPLIT * j + piece, far_p[piece], farvec)
        if j > MOBA_TOPK:
            gt = lax.dot_general(ks, q_ref[rows, :], nt, preferred_element_type=F32)
            gate = gt[0:nb] + gt[nb:2 * nb] + gt[2 * nb:3 * nb]
            rank = jnp.zeros((nb, blk), jnp.int32)
            for mm in range(j):
                gm = gate[mm:mm + 1, :]
                beats = (gm > gate) | ((gm == gate) & (mm < n_idx))
                rank = rank + beats.astype(jnp.int32)
            drop_t = ((rank >= MOBA_TOPK) & (n_idx < j)).astype(F32)
            drop = jnp.concatenate([drop_t, jnp.zeros((hd - nb, blk), F32)], axis=0).T
            qx_ref[rows, hd:] = (drop + farvec).astype(BF16)
        else:
            qx_ref[rows, hd:] = jnp.broadcast_to(farvec, (blk, hd)).astype(BF16)

    for n in range(nb):
        s = lax.dot_general(qx_ref[n * blk:, :], kx_ref[n * blk:(n + 1) * blk, :], nt,
                            preferred_element_type=F32)
        for j in range(n, nb):
            s_ref[_score_slot(j, n)] = s[(j - n) * blk:(j - n + 1) * blk, :]


def _attn_outputs(bias_ref, vx_ref, s_ref, o_ref):
    blk = MOBA_BLOCK
    seq, hd = o_ref.shape
    for j in reversed(range(seq // blk)):
        logits = []
        for n in range(j + 1):
            lg = s_ref[_score_slot(j, n)]
            if n == j:
                lg = lg + bias_ref[1]
            elif n == j - 1:
                lg = lg + bias_ref[0]
            logits.append(lg)
        mx = jnp.max(functools.reduce(jnp.maximum, logits), axis=-1, keepdims=True)
        p = jnp.concatenate([jnp.exp2(lg - mx).astype(BF16) for lg in logits], axis=-1)
        acc = jnp.dot(p, vx_ref[0:(j + 1) * blk, :], preferred_element_type=F32)
        o_ref[j * blk:(j + 1) * blk, :] = (acc[:, 0:hd] / acc[:, hd:]).astype(BF16)


def _attn_kernel(tab_ref, q_ref, k_ref, v_ref, bias_ref, kf_ref, o_ref, kx_ref, vx_ref, qx_ref, s_ref):
    def cols(hh):
        return slice(hh * HEAD_DIM, (hh + 1) * HEAD_DIM)

    def scores(hh):
        _attn_scores(tab_ref, pl.program_id(1) * ATTN_HEADS_PER_STEP + hh, q_ref.at[:, cols(hh)],
                     k_ref.at[:, cols(hh)], v_ref.at[:, cols(hh)], kf_ref,
                     kx_ref.at[hh % 2], vx_ref.at[hh % 2], qx_ref.at[hh % 2], s_ref.at[hh % 2])

    scores(0)
    for hh in range(ATTN_HEADS_PER_STEP):
        if hh + 1 < ATTN_HEADS_PER_STEP:
            scores(hh + 1)
        _attn_outputs(bias_ref.at[hh], vx_ref.at[hh % 2], s_ref.at[hh % 2], o_ref.at[:, cols(hh)])


def _attn(qkv, bias, rel_table, seq):
    m, n3 = qkv.shape
    d = n3 // 3
    hps = ATTN_HEADS_PER_STEP
    groups = d // (hps * HEAD_DIM)
    blk = MOBA_BLOCK
    assert seq % blk == 0
    head_cols = pl.BlockSpec((seq, hps * HEAD_DIM), lambda b, g: (b, g))
    return pl.pallas_call(
        _attn_kernel,
        grid=(m // seq, groups),
        in_specs=[pl.BlockSpec(memory_space=pltpu.SMEM),
                  head_cols,
                  pl.BlockSpec((seq, hps * HEAD_DIM), lambda b, g: (b, groups + g)),
                  pl.BlockSpec((seq, hps * HEAD_DIM), lambda b, g: (b, 2 * groups + g)),
                  pl.BlockSpec((hps, 2, blk, blk), lambda b, g: (g, 0, 0, 0)),
                  pl.BlockSpec((seq, HEAD_DIM), lambda b, g: (0, 0))],
        out_specs=head_cols,
        out_shape=jax.ShapeDtypeStruct((m, d), BF16),
        scratch_shapes=[pltpu.VMEM((2, seq, 2 * HEAD_DIM), BF16),
                        pltpu.VMEM((2, seq, 2 * HEAD_DIM), BF16),
                        pltpu.VMEM((2, seq, 2 * HEAD_DIM), BF16),
                        pltpu.VMEM((2, _score_slot(seq // blk, 0), blk, blk), F32)],
        compiler_params=_cparams(2),
        name="attn",
    )(rel_table, qkv, qkv, qkv, bias, _key_features(seq))


def kernel(x, c, norm_g, w_ada, b_ada, pool_w, pool_scale, w_qkv, w_o, rel_table,
           w_up, conv_w, conv_b, w_down, final_g):
    bsz, seq, d = x.shape
    depth = w_ada.shape[0]
    m = bsz * seq

    mod = _ada(c, w_ada, b_ada)
    bias = _relbias(rel_table)

    norm_g3 = norm_g.reshape(depth * 2, 1, d)
    pool_scale3 = pool_scale.reshape(-1, 1, d)
    conv_b3 = conv_b.reshape(depth, 1, -1)
    final_g3 = final_g.reshape(1, 1, d)

    assert seq % CHUNK == 0
    xf = _to_chunk_order(x).reshape(m, d)
    h = None
    out = None
    for i in range(depth):
        li = i // N_MIXERS
        if i % N_MIXERS == 0:
            xf, h2 = _pool(xf, mod, norm_g3, pool_w, pool_scale3, i, li, seq)
        else:
            o = _attn(_qkv(h, w_qkv, li), bias, rel_table, seq)
            xf, h2 = _mm_res(o, w_o, li, xf, mod, (i, 2), seq, mode="xh",
                             norm=norm_g3, norm_idx=i * 2 + 1, next_layer=(i, 3))
        a, w_down_b = _up(h2, w_up, conv_w, conv_b3, w_down, i, seq)
        if i == depth - 1:
            (out,) = _mm_res(a, w_down_b, 0, xf, mod, (i, 5), seq, mode="final",
                             norm=final_g3, norm_idx=0)
        elif (i + 1) % N_MIXERS == 0:
            (xf,) = _mm_res(a, w_down_b, 0, xf, mod, (i, 5), seq, mode="x")
        else:
            xf, h = _mm_res(a, w_down_b, 0, xf, mod, (i, 5), seq, mode="xh",
                            norm=norm_g3, norm_idx=(i + 1) * 2, next_layer=(i + 1, 0))
    return _to_chunk_order(out.reshape(bsz, seq, d), inverse=True)
```
